```python
import jax
import jax.numpy as jnp
from jax import lax
import numpy as np

D_MODEL = 2048
BATCH = 4
SEQ = 2048
DEPTH = 4
DEC_BATCH = 128
DEC_SEQ = 8
PAST_LEN = 8192
PAGE_SIZE = 128

N_A = DEPTH // 2
N_B = DEPTH - N_A
RW_HEAD = 64
RW_HEADS = D_MODEL // RW_HEAD
RW_DECAY_LORA = max(32, int(round(1.8 * D_MODEL ** 0.5 / 32)) * 32)
RW_AAA_LORA = max(32, int(round(1.8 * D_MODEL ** 0.5 / 32)) * 32)
RW_MV_LORA = max(32, int(round(1.3 * D_MODEL ** 0.5 / 32)) * 32)
RW_GATE_LORA = max(32, int(round(0.6 * D_MODEL ** 0.8 / 32)) * 32)
GN_EPS = RW_HEAD * 1e-5
D_FF = ((8 * D_MODEL // 3) + 255) // 256 * 256
CONV_W = 3
MLA_HEADS = 16
D_C = 512
D_QC = 512
D_NOPE = 128
D_ROPE = 64
D_QK = D_NOPE + D_ROPE
D_V = 128
ROPE_BASE = 10000.0
ATTN_SCALE = D_QK ** -0.5
Q_BLOCK = 128
RMS_EPS = 1e-6

kernel_name = 'yoco_rwkv7_mla_convffn_step'

F32 = jnp.float32


def rms_norm(x, g):
    xf = x.astype(F32)
    y = xf * lax.rsqrt(jnp.mean(xf * xf, axis=-1, keepdims=True) + RMS_EPS)
    return (y * g.astype(F32)).astype(x.dtype)


def rope(x, pos):
    half = x.shape[-1] // 2
    inv = ROPE_BASE ** (-jnp.arange(half, dtype=F32) / half)
    ang = pos.astype(F32)[:, None] * inv[None, :]
    cos = jnp.cos(ang)[None, :, None, :]
    sin = jnp.sin(ang)[None, :, None, :]
    xf = x.astype(F32)
    x1, x2 = xf[..., :half], xf[..., half:]
    return jnp.concatenate([x1 * cos - x2 * sin, x1 * sin + x2 * cos], axis=-1).astype(x.dtype)


def conv_ffn(x, prev, g_norm, w_gate, w_up, conv_w, conv_b, w_down):
    h = rms_norm(x, g_norm)
    u = h @ w_gate
    t = x.shape[1]
    ext = jnp.concatenate([prev.astype(u.dtype), u], axis=1)
    conv = conv_b
    for j in range(CONV_W):
        conv = conv + conv_w[j] * ext[:, j:j + t]
    y = (jax.nn.silu(conv) * (h @ w_up)) @ w_down
    return x + y, ext[:, -(CONV_W - 1):]


def wkv7_scan(s0, r, w, k, v, a, b):
    def step(s, inp):
        r_t, w_t, k_t, v_t, a_t, b_t = inp
        sa = jnp.einsum('bhij,bhj->bhi', s, a_t)
        s = s * w_t[:, :, None, :] + sa[..., None] * b_t[:, :, None, :] + v_t[..., None] * k_t[:, :, None, :]
        return s, jnp.einsum('bhij,bhj->bhi', s, r_t)
    xs = tuple(jnp.moveaxis(t, 1, 0) for t in (r, w, k, v, a, b))
    s, y = lax.scan(step, s0, xs)
    return s, jnp.moveaxis(y, 0, 1)


def rwkv7_time_mix(x, shift_prev, wkv_prev, v_first, v_lora, g_norm, mix, w0, w1, w2,
                   a0, a1, a2, g1, g2, k_k, k_a, r_k, wr, wk, wv, ln_w, ln_b, wo):
    bsz, t, d = x.shape
    h = rms_norm(x, g_norm)
    prev = jnp.concatenate([shift_prev[:, None, :].astype(h.dtype), h[:, :-1]], axis=1)
    xx = prev - h
    xr, xw, xk, xv, xa, xg = (h + xx * mix[i] for i in range(6))
    r = xr @ wr
    k = xk @ wk
    v = xv @ wv
    w_log = -jax.nn.softplus(-(w0 + jnp.tanh(xw @ w1) @ w2).astype(F32)) - 0.5
    decay = jnp.exp(-jnp.exp(w_log))
    a = jax.nn.sigmoid((a0 + (xa @ a1) @ a2).astype(F32))
    g = jax.nn.sigmoid(xg @ g1) @ g2
    if v_lora is None:
        v_first = v
    else:
        v0, v1, v2 = v_lora
        v = v + (v_first - v) * jax.nn.sigmoid(v0 + (xv @ v1) @ v2)
    hd = lambda z: z.reshape(bsz, t, RW_HEADS, RW_HEAD).astype(F32)
    kk = hd(k * k_k)
    kk = kk / jnp.maximum(jnp.sqrt(jnp.sum(kk * kk, axis=-1, keepdims=True)), 1e-12)
    k = k * (1 + (a - 1) * k_a)
    rh, kh, vh, ah = hd(r), hd(k), hd(v), hd(a)
    s, y = wkv7_scan(wkv_prev.astype(F32), rh, hd(decay), kh, vh, -kk, kk * ah)
    mu = jnp.mean(y, axis=-1, keepdims=True)
    var = jnp.mean((y - mu) ** 2, axis=-1, keepdims=True)
    y = ((y - mu) * lax.rsqrt(var + GN_EPS)).reshape(bsz, t, d) * ln_w + ln_b
    y = y + (jnp.sum(rh * kh * r_k, axis=-1, keepdims=True) * vh).reshape(bsz, t, d)
    out = (y * g).astype(x.dtype) @ wo
    return x + out, h[:, -1], s.astype(wkv_prev.dtype), v_first


def mla_kv_side(x, pos, g_in, w_dkv, g_c, w_kr):
    u = rms_norm(x, g_in)
    c = rms_norm(u @ w_dkv, g_c)
    k_pe = rope((u @ w_kr)[:, :, None, :], pos)[:, :, 0, :]
    return c, k_pe


def key_scale(c, k_pe, w_uk):
    k_nope = jnp.einsum('bkc,chd->bkhd', c.astype(F32), w_uk.astype(F32))
    ss = jnp.sum(k_nope * k_nope, axis=-1) + jnp.sum(jnp.square(k_pe.astype(F32)), axis=-1)[..., None]
    return lax.rsqrt(ss / D_QK + RMS_EPS)


def causal_block_attention(q, k, v):
    bsz, s_len, nh, _ = q.shape
    nb = s_len // Q_BLOCK
    q_blocks = jnp.moveaxis(q.reshape(bsz, nb, Q_BLOCK, nh, D_QK), 1, 0)
    k_pos = jnp.arange(s_len)

    def block(args):
        q_blk, i = args
        s = jnp.einsum('bqhd,bkhd->bhqk', q_blk, k).astype(F32) * ATTN_SCALE
        q_pos = i * Q_BLOCK + jnp.arange(Q_BLOCK)
        s = jnp.where(k_pos[None, :] <= q_pos[:, None], s, -jnp.inf)
        p = jax.nn.softmax(s, axis=-1)
        return jnp.einsum('bhqk,bkhd->bqhd', p.astype(v.dtype), v)

    out = lax.map(block, (q_blocks, jnp.arange(nb)))
    return jnp.moveaxis(out, 0, 1).reshape(bsz, s_len, nh, D_V)


def prompt_attend_fn(c, k_pe, w_uk, w_uv, g_k):
    bsz, s_len = c.shape[:2]
    k_nope = jnp.einsum('btc,chd->bthd', c, w_uk)
    k = jnp.concatenate([k_nope, jnp.broadcast_to(k_pe[:, :, None, :], (bsz, s_len, MLA_HEADS, D_ROPE)).astype(k_nope.dtype)], axis=-1)
    k = rms_norm(k, g_k)
    v = jnp.einsum('btc,chd->bthd', c, w_uv)
    return lambda q: causal_block_attention(q, k, v)


def sample_attend_fn(c_new, kpe_new, cache_c, cache_kpe, page_table, w_uk, w_uv, g_k):
    s_new = key_scale(c_new, kpe_new, w_uk)
    s_past = lax.map(lambda pidx: key_scale(cache_c[pidx], cache_kpe[pidx], w_uk), page_table.T)
    g_kn, g_kr = g_k[:D_NOPE].astype(F32), g_k[D_NOPE:].astype(F32)
    w_uk32, w_uv32 = w_uk.astype(F32), w_uv.astype(F32)

    def attend(q):
        qf = q.astype(F32)
        q_lat = jnp.einsum('bthd,chd->bthc', qf[..., :D_NOPE] * g_kn, w_uk32)
        q_r = qf[..., D_NOPE:] * g_kr

        def logits(c, kpe, s):
            sc = jnp.einsum('bthc,bkc->bhtk', q_lat, c.astype(F32)) + jnp.einsum('bthr,bkr->bhtk', q_r, kpe.astype(F32))
            return sc * (jnp.swapaxes(s, 1, 2)[:, :, None, :] * ATTN_SCALE)

        t = q.shape[1]
        c_new32 = c_new.astype(F32)
        sc = jnp.where(jnp.tril(jnp.ones((t, t), dtype=bool)), logits(c_new32, kpe_new, s_new), -jnp.inf)
        m = jnp.max(sc, axis=-1)
        p = jnp.exp(sc - m[..., None])
        l = jnp.sum(p, axis=-1)
        acc = jnp.einsum('bhtk,bkc->bhtc', p, c_new32)

        def page_step(carry, inp):
            m, l, acc = carry
            pidx, s_pg = inp
            c_pg = cache_c[pidx].astype(F32)
            sc = logits(c_pg, cache_kpe[pidx], s_pg)
            m_new = jnp.maximum(m, jnp.max(sc, axis=-1))
            alpha = jnp.exp(m - m_new)
            p = jnp.exp(sc - m_new[..., None])
            return (m_new, l * alpha + jnp.sum(p, axis=-1), acc * alpha[..., None] + jnp.einsum('bhtk,bkc->bhtc', p, c_pg)), None

        (m, l, acc), _ = lax.scan(page_step, (m, l, acc), (page_table.T, s_past))
        o_lat = acc / l[..., None]
        return jnp.einsum('bhtc,chd->bthd', o_lat, w_uv32).astype(q.dtype)

    return attend


def mla_layer(x, pos, g_norm, w_dq, g_cq, w_uq, g_q, wo, attend):
    bsz, t, _ = x.shape
    h = rms_norm(x, g_norm)
    cq = rms_norm(h @ w_dq, g_cq)
    q = jnp.einsum('btc,chd->bthd', cq, w_uq)
    q = jnp.concatenate([q[..., :D_NOPE], rope(q[..., D_NOPE:], pos)], axis=-1)
    q = rms_norm(q, g_q)
    o = attend(q)
    return x + o.reshape(bsz, t, MLA_HEADS * D_V) @ wo


def _normal(key, shape, scale):
    return scale * jax.random.normal(key, shape, F32)


def _gain(key, shape):
    return 1.0 + 0.05 * jax.random.normal(key, shape, F32)


def setup_inputs(seed: int = 0) -> dict:
    key = jax.random.key(seed)
    ks = list(jax.random.split(key, 64))
    nk = ks.pop
    d = D_MODEL
    n_pages = PAST_LEN // PAGE_SIZE
    n_used = DEC_BATCH * n_pages
    n_pool = (5 * n_used + 3) // 4
    n_v = max(N_A - 1, 0)
    inp = {}
    inp['x_prompt'] = _normal(nk(), (BATCH, SEQ, d), 1.0)
    inp['x_sample'] = _normal(nk(), (DEC_BATCH, DEC_SEQ, d), 1.0)
    inp['cache_kv_latent'] = _normal(nk(), (n_pool, PAGE_SIZE, D_C), 1.0)
    inp['cache_k_pe'] = _normal(nk(), (n_pool, PAGE_SIZE, D_ROPE), 1.0)
    inp['state_wkv'] = _normal(nk(), (N_A, DEC_BATCH, RW_HEADS, RW_HEAD, RW_HEAD), 1.0)
    inp['state_shift'] = _normal(nk(), (N_A, DEC_BATCH, d), 1.0)
    inp['state_conv'] = _normal(nk(), (DEPTH, DEC_BATCH, CONV_W - 1, D_FF), 1.0)
    perm = jax.random.permutation(nk(), n_pool)[:n_used]
    inp['page_table'] = perm.reshape(DEC_BATCH, n_pages).astype(jnp.int32)
    inp['ffn_norm'] = _gain(nk(), (DEPTH, d))
    inp['ffn_w_gate'] = _normal(nk(), (DEPTH, d, D_FF), d ** -0.5)
    inp['ffn_w_up'] = _normal(nk(), (DEPTH, d, D_FF), d ** -0.5)
    inp['ffn_conv_w'] = _normal(nk(), (DEPTH, CONV_W, D_FF), CONV_W ** -0.5)
    inp['ffn_conv_b'] = _normal(nk(), (DEPTH, D_FF), 0.01)
    inp['ffn_w_down'] = _normal(nk(), (DEPTH, D_FF, d), D_FF ** -0.5)
    inp['rw_norm'] = _gain(nk(), (N_A, d))
    inp['rw_mix'] = jax.random.uniform(nk(), (N_A, 6, d), F32)
    inp['rw_w0'] = jax.random.uniform(nk(), (N_A, d), F32, minval=-6.0, maxval=-1.0)
    inp['rw_w1'] = _normal(nk(), (N_A, d, RW_DECAY_LORA), d ** -0.5)
    inp['rw_w2'] = _normal(nk(), (N_A, RW_DECAY_LORA, d), 0.5 * RW_DECAY_LORA ** -0.5)
    inp['rw_a0'] = _normal(nk(), (N_A, d), 0.5)
    inp['rw_a1'] = _normal(nk(), (N_A, d, RW_AAA_LORA), d ** -0.5)
    inp['rw_a2'] = _normal(nk(), (N_A, RW_AAA_LORA, d), RW_AAA_LORA ** -0.5)
    inp['rw_v0'] = _normal(nk(), (n_v, d), 0.5)
    inp['rw_v1'] = _normal(nk(), (n_v, d, RW_MV_LORA), d ** -0.5)
    inp['rw_v2'] = _normal(nk(), (n_v, RW_MV_LORA, d), RW_MV_LORA ** -0.5)
    inp['rw_g1'] = _normal(nk(), (N_A, d, RW_GATE_LORA), d ** -0.5)
    inp['rw_g2'] = _normal(nk(), (N_A, RW_GATE_LORA, d), RW_GATE_LORA ** -0.5)
    inp['rw_k_k'] = 0.85 + _normal(nk(), (N_A, d), 0.1)
    inp['rw_k_a'] = 1.0 + _normal(nk(), (N_A, d), 0.1)
    inp['rw_r_k'] = _normal(nk(), (N_A, RW_HEADS, RW_HEAD), 0.1)
    inp['rw_wr'] = _normal(nk(), (N_A, d, d), d ** -0.5)
    inp['rw_wk'] = _normal(nk(), (N_A, d, d), d ** -0.5)
    inp['rw_wv'] = _normal(nk(), (N_A, d, d), d ** -0.5)
    inp['rw_ln_w'] = _gain(nk(), (N_A, d))
    inp['rw_ln_b'] = _normal(nk(), (N_A, d), 0.01)
    inp['rw_wo'] = _normal(nk(), (N_A, d, d), d ** -0.5)
    inp['kv_norm'] = _gain(nk(), (d,))
    inp['kv_w_dkv'] = _normal(nk(), (d, D_C), d ** -0.5)
    inp['kv_norm_c'] = _gain(nk(), (D_C,))
    inp['kv_w_kr'] = _normal(nk(), (d, D_ROPE), d ** -0.5)
    inp['kv_w_uk'] = _normal(nk(), (D_C, MLA_HEADS, D_NOPE), D_C ** -0.5)
    inp['kv_w_uv'] = _normal(nk(), (D_C, MLA_HEADS, D_V), D_C ** -0.5)
    inp['kv_norm_k'] = _gain(nk(), (D_QK,))
    inp['mla_norm'] = _gain(nk(), (N_B, d))
    inp['mla_w_dq'] = _normal(nk(), (N_B, d, D_QC), d ** -0.5)
    inp['mla_norm_cq'] = _gain(nk(), (N_B, D_QC))
    inp['mla_w_uq'] = _normal(nk(), (N_B, D_QC, MLA_HEADS, D_QK), D_QC ** -0.5)
    inp['mla_norm_q'] = _gain(nk(), (N_B, D_QK))
    inp['mla_wo'] = _normal(nk(), (N_B, MLA_HEADS * D_V, d), (MLA_HEADS * D_V) ** -0.5)
    return inp


def reference(x_prompt, x_sample, cache_kv_latent, cache_k_pe, state_wkv, state_shift, state_conv, page_table,
              ffn_norm, ffn_w_gate, ffn_w_up, ffn_conv_w, ffn_conv_b, ffn_w_down,
              rw_norm, rw_mix, rw_w0, rw_w1, rw_w2, rw_a0, rw_a1, rw_a2, rw_v0, rw_v1, rw_v2,
              rw_g1, rw_g2, rw_k_k, rw_k_a, rw_r_k, rw_wr, rw_wk, rw_wv, rw_ln_w, rw_ln_b, rw_wo,
              kv_norm, kv_w_dkv, kv_norm_c, kv_w_kr, kv_w_uk, kv_w_uv, kv_norm_k,
              mla_norm, mla_w_dq, mla_norm_cq, mla_w_uq, mla_norm_q, mla_wo):

    def trunk(x, pos, shift_in, wkv_in, conv_in, make_attend):
        v_first = None
        attend = None
        c = k_pe = None
        shifts, wkvs, convs = [], [], []
        for layer in range(DEPTH):
            if layer < N_A:
                v_lora = None if layer == 0 else (rw_v0[layer - 1], rw_v1[layer - 1], rw_v2[layer - 1])
                x, sh, s, v_first = rwkv7_time_mix(
                    x, shift_in[layer], wkv_in[layer], v_first, v_lora,
                    rw_norm[layer], rw_mix[layer], rw_w0[layer], rw_w1[layer], rw_w2[layer],
                    rw_a0[layer], rw_a1[layer], rw_a2[layer], rw_g1[layer], rw_g2[layer],
                    rw_k_k[layer], rw_k_a[layer], rw_r_k[layer], rw_wr[layer], rw_wk[layer], rw_wv[layer],
                    rw_ln_w[layer], rw_ln_b[layer], rw_wo[layer])
                shifts.append(sh)
                wkvs.append(s)
            else:
                if layer == N_A:
                    c, k_pe = mla_kv_side(x, pos, kv_norm, kv_w_dkv, kv_norm_c, kv_w_kr)
                    attend = make_attend(c, k_pe)
                b = layer - N_A
                x = mla_layer(x, pos, mla_norm[b], mla_w_dq[b], mla_norm_cq[b], mla_w_uq[b], mla_norm_q[b], mla_wo[b], attend)
            x, cv = conv_ffn(x, conv_in[layer], ffn_norm[layer], ffn_w_gate[layer], ffn_w_up[layer],
                             ffn_conv_w[layer], ffn_conv_b[layer], ffn_w_down[layer])
            convs.append(cv)
        return x, c, k_pe, jnp.stack(wkvs), jnp.stack(shifts), jnp.stack(convs)

    bp, sp, d = x_prompt.shape
    pos_p = jnp.arange(sp)
    zero_shift = jnp.zeros((N_A, bp, d), x_prompt.dtype)
    zero_wkv = jnp.zeros((N_A, bp, RW_HEADS, RW_HEAD, RW_HEAD), state_wkv.dtype)
    zero_conv = jnp.zeros((DEPTH, bp, CONV_W - 1, D_FF), x_prompt.dtype)
    y_prompt, kv_latent_prompt, k_pe_prompt, wkv_prompt, shift_prompt, conv_prompt = trunk(
        x_prompt, pos_p, zero_shift, zero_wkv, zero_conv,
        lambda c, kp: prompt_attend_fn(c, kp, kv_w_uk, kv_w_uv, kv_norm_k))

    past_len = page_table.shape[1] * cache_kv_latent.shape[1]
    pos_s = past_len + jnp.arange(x_sample.shape[1])
    y_sample, kv_latent_sample, k_pe_sample, wkv_sample, shift_sample, conv_sample = trunk(
        x_sample, pos_s, state_shift, state_wkv, state_conv,
        lambda c, kp: sample_attend_fn(c, kp, cache_kv_latent, cache_k_pe, page_table, kv_w_uk, kv_w_uv, kv_norm_k))

    return (y_prompt, y_sample,
            kv_latent_prompt, k_pe_prompt, wkv_prompt, shift_prompt, conv_prompt,
            kv_latent_sample, k_pe_sample, wkv_sample, shift_sample, conv_sample)
```

```python
import functools

import jax
import jax.numpy as jnp
from jax import lax
from jax.experimental import pallas as pl
from jax.experimental.pallas import tpu as pltpu

F32 = jnp.float32
BF16 = jnp.bfloat16

RMS_EPS = 1e-6
ROPE_BASE = 10000.0
GN_EPS_PER_CHANNEL = 1e-5
CONV_W = 3

VMEM_LIMIT_BYTES = 52 * 1024 * 1024
LANES = 128
PAGES_PER_STEP = 8


def _pick(n, candidates):
    for c in candidates:
        if c <= n and n % c == 0:
            return c
    return n


def _cparams(sem):
    return pltpu.CompilerParams(dimension_semantics=sem, vmem_limit_bytes=VMEM_LIMIT_BYTES)


def _mm_kernel(x_ref, w_ref, o_ref):
    o_ref[0] = jnp.dot(x_ref[0].astype(BF16), w_ref[0],
                       preferred_element_type=F32).astype(o_ref.dtype)


def bmm(x, w, out_dtype=F32):
    g, m, k = x.shape
    n = w.shape[2]
    bm = _pick(m, (1024, 512, 256, 128, 64, 32, 16, 8))
    bn = n if n <= 1024 else _pick(n, (512, 256, 128))
    return pl.pallas_call(
        _mm_kernel,
        grid=(g, m // bm, n // bn),
        in_specs=[pl.BlockSpec((1, bm, k), lambda a, i, j: (a, i, 0)),
                  pl.BlockSpec((1, k, bn), lambda a, i, j: (a, 0, j))],
        out_specs=pl.BlockSpec((1, bm, bn), lambda a, i, j: (a, i, j)),
        out_shape=jax.ShapeDtypeStruct((g, m, n), out_dtype),
        compiler_params=_cparams(("parallel", "parallel", "arbitrary")),
        name="bmm",
    )(x, w)


def mm(x, w, out_dtype=F32):
    lead = x.shape[:-1]
    out = bmm(x.reshape((1, -1, x.shape[-1])), w[None], out_dtype)
    return out.reshape(lead + (w.shape[1],))


def _wkv_kernel(r_ref, w_ref, k_ref, v_ref, a_ref, b_ref, s0_ref, y_ref, s_ref, *, tc, n, ig):
    @pl.when(pl.program_id(1) == 0)
    def _():
        s_ref[...] = s0_ref[...]

    for grp in range(n // ig):
        i0 = grp * ig

        def step(t, carry, i0=i0):
            a_t = a_ref[0, t]
            w_t = w_ref[0, t]
            b_t = b_ref[0, t]
            k_t = k_ref[0, t]
            r_t = r_ref[0, t]
            v_t = v_ref[0, t, i0:i0 + ig, :]
            new, ys = [], []
            for ii in range(ig):
                s = carry[ii]
                sa = jnp.sum(s * a_t, axis=0, keepdims=True)
                s = s * w_t + sa * b_t + v_t[ii:ii + 1] * k_t
                ys.append(jnp.sum(s * r_t, axis=0, keepdims=True))
                new.append(s)
            y_ref[0, t, i0:i0 + ig, :] = jnp.concatenate(ys, axis=0)
            return tuple(new)

        init = tuple(s_ref[0, i0 + ii] for ii in range(ig))
        out = lax.fori_loop(0, tc, step, init)
        for ii in range(ig):
            s_ref[0, i0 + ii] = out[ii]


def wkv_scan(r, w, k, v, a, b, s0):
    g, t, n, l = r.shape
    tc = _pick(t, (64, 32, 16, 8))
    seq_spec = pl.BlockSpec((1, tc, n, l), lambda a_, c: (a_, c, 0, 0))
    st_spec = pl.BlockSpec((1, n, n, l), lambda a_, c: (a_, 0, 0, 0))
    return pl.pallas_call(
        functools.partial(_wkv_kernel, tc=tc, n=n, ig=4),
        grid=(g, t // tc),
        in_specs=[seq_spec] * 6 + [st_spec],
        out_specs=[seq_spec, st_spec],
        out_shape=[jax.ShapeDtypeStruct((g, t, n, l), F32),
                   jax.ShapeDtypeStruct((g, n, n, l), F32)],
        compiler_params=_cparams(("parallel", "arbitrary")),
        name="wkv_scan",
    )(r, w, k, v, a, b, s0)


def _flash_kernel(q_ref, k_ref, v_ref, o_ref, *, bq, bk, scale):
    qi = pl.program_id(2)
    q = q_ref[0, 0]
    dv = v_ref.shape[-1]

    def body(j, carry):
        m, l, acc = carry
        off = pl.multiple_of(j * bk, bk)
        kj = k_ref[0, 0, pl.ds(off, bk), :]
        vj = v_ref[0, pl.ds(off, bk), :]
        s = lax.dot_general(q, kj, (((1,), (1,)), ((), ())), preferred_element_type=F32) * scale
        qpos = qi * bq + lax.broadcasted_iota(jnp.int32, (bq, bk), 0)
        kpos = j * bk + lax.broadcasted_iota(jnp.int32, (bq, bk), 1)
        s = jnp.where(kpos <= qpos, s, -jnp.inf)
        m_new = jnp.maximum(m, jnp.max(s, axis=-1, keepdims=True))
        alpha = jnp.exp(m - m_new)
        p = jnp.exp(s - m_new)
        l = l * alpha + jnp.sum(p, axis=-1, keepdims=True)
        acc = acc * alpha + jnp.dot(p.astype(BF16), vj, preferred_element_type=F32)
        return m_new, l, acc

    n_kv = (qi * bq + bq + bk - 1) // bk
    init = (jnp.full((bq, 1), -jnp.inf, F32), jnp.zeros((bq, 1), F32), jnp.zeros((bq, dv), F32))
    m, l, acc = lax.fori_loop(0, n_kv, body, init)
    o_ref[0] = acc / l


def flash_prompt(q, k, v, n_heads, scale):
    bsz, _, s_len, dqk = q.shape
    dv = v.shape[-1] // n_heads
    bq = _pick(s_len, (256, 128, 64, 32, 16, 8))
    bk = bq
    return pl.pallas_call(
        functools.partial(_flash_kernel, bq=bq, bk=bk, scale=scale),
        grid=(bsz, n_heads, s_len // bq),
        in_specs=[pl.BlockSpec((1, 1, bq, dqk), lambda b, h, i: (b, h, i, 0)),
                  pl.BlockSpec((1, 1, s_len, dqk), lambda b, h, i: (b, h, 0, 0)),
                  pl.BlockSpec((1, s_len, dv), lambda b, h, i: (b, 0, h))],
        out_specs=pl.BlockSpec((1, bq, dv), lambda b, h, i: (b, i, h)),
        out_shape=jax.ShapeDtypeStruct((bsz, s_len, n_heads * dv), F32),
        compiler_params=_cparams(("parallel", "parallel", "arbitrary")),
        name="flash_prompt",
    )(q, k, v)


def _sum_rows_to_lanes(x):
    hi = x.astype(BF16)
    lo = (x - hi.astype(F32)).astype(BF16)
    ones = jnp.ones((8, x.shape[1]), BF16)
    dn = (((1,), (1,)), ((), ()))
    out = (lax.dot_general(ones, hi, dn, preferred_element_type=F32)
           + lax.dot_general(ones, lo, dn, preferred_element_type=F32))
    return out[0:1]


def _key_scale_block(wt, c, kpe, n_heads, d_qk):
    k_t = lax.dot_general(wt, c.astype(BF16), (((1,), (1,)), ((), ())), preferred_element_type=F32)
    sq = k_t * k_t
    ss = jnp.sum(sq.reshape(n_heads, sq.shape[0] // n_heads, sq.shape[1]), axis=1)
    kpe = kpe.astype(F32)
    pe = _sum_rows_to_lanes(kpe * kpe)
    return lax.rsqrt((ss + pe) / d_qk + RMS_EPS)


def _kscale_kernel(pt_ref, *refs, npg, n_heads, d_qk):
    del pt_ref
    c_refs = refs[:npg]
    kpe_refs = refs[npg:2 * npg]
    wt_ref = refs[2 * npg]
    o_ref = refs[2 * npg + 1]
    page = c_refs[0].shape[1]
    pair = 2 if npg % 2 == 0 else 1
    for p in range(0, npg, pair):
        c = jnp.concatenate([c_refs[p + q][0] for q in range(pair)], axis=0)
        kpe = jnp.concatenate([kpe_refs[p + q][0] for q in range(pair)], axis=0)
        o_ref[0, :, p * page:(p + pair) * page] = _key_scale_block(wt_ref[...], c, kpe, n_heads, d_qk)


def paged_key_scale(cache_c, cache_kpe, page_table_flat, w_uk_t, n_heads, d_qk, npg):
    n_used = page_table_flat.shape[0]
    page, d_c = cache_c.shape[1:]
    d_r = cache_kpe.shape[2]

    def page_spec(width, p):
        return pl.BlockSpec((1, page, width), lambda g, pt, p=p: (pt[g * npg + p], 0, 0))

    in_specs = ([page_spec(d_c, p) for p in range(npg)] + [page_spec(d_r, p) for p in range(npg)]
                + [pl.BlockSpec(w_uk_t.shape, lambda g, pt: (0, 0))])
    grid_spec = pltpu.PrefetchScalarGridSpec(
        num_scalar_prefetch=1, grid=(n_used // npg,), in_specs=in_specs,
        out_specs=pl.BlockSpec((1, n_heads, npg * page), lambda g, pt: (g, 0, 0)))
    return pl.pallas_call(
        functools.partial(_kscale_kernel, npg=npg, n_heads=n_heads, d_qk=d_qk),
        grid_spec=grid_spec,
        out_shape=jax.ShapeDtypeStruct((n_used // npg, n_heads, npg * page), F32),
        compiler_params=_cparams(("arbitrary",)),
        name="paged_key_scale",
    )(page_table_flat, *([cache_c] * npg), *([cache_kpe] * npg), w_uk_t)


def _paged_attn_kernel(pt_ref, qlat_ref, qr_ref, cnew_ref, kpenew_ref, snew_ref, *refs,
                       npg, n_heads, t_len, scale):
    del pt_ref
    c_refs = refs[:npg]
    kpe_refs = refs[npg:2 * npg]
    s_ref = refs[2 * npg]
    o_ref = refs[2 * npg + 1]
    m_sc, l_sc, acc_sc = refs[2 * npg + 2:]
    g = pl.program_id(1)
    rows = n_heads * t_len
    dn = (((1,), (1,)), ((), ()))
    qlat = qlat_ref[0]
    qr = qr_ref[0]

    def process(c, kpe, key_scale, mask):
        c = c.astype(BF16)
        nk = c.shape[0]
        sc = (lax.dot_general(qlat, c, dn, preferred_element_type=F32)
              + lax.dot_general(qr, kpe.astype(BF16), dn, preferred_element_type=F32))
        row_scale = jnp.broadcast_to(key_scale[:, None, :], (n_heads, t_len, nk)).reshape(rows, nk)
        sc = sc * (row_scale * scale)
        if mask is not None:
            sc = jnp.where(mask, sc, -jnp.inf)
        m = m_sc[...]
        m_new = jnp.maximum(m, jnp.max(sc, axis=-1, keepdims=True))
        alpha = jnp.exp(m - m_new)
        p = jnp.exp(sc - m_new)
        l_sc[...] = l_sc[...] * alpha + jnp.sum(p, axis=-1, keepdims=True)
        acc_sc[...] = acc_sc[...] * alpha + jnp.dot(p.astype(BF16), c, preferred_element_type=F32)
        m_sc[...] = m_new

    @pl.when(g == 0)
    def _():
        m_sc[...] = jnp.full(m_sc.shape, -jnp.inf, F32)
        l_sc[...] = jnp.zeros(l_sc.shape, F32)
        acc_sc[...] = jnp.zeros(acc_sc.shape, F32)
        nk = cnew_ref.shape[1]
        t_row = lax.broadcasted_iota(jnp.int32, (rows, nk), 0) % t_len
        key = lax.broadcasted_iota(jnp.int32, (rows, nk), 1)
        process(cnew_ref[0], kpenew_ref[0], snew_ref[0], (key <= t_row) & (key < t_len))

    c = jnp.concatenate([r[0] for r in c_refs], axis=0)
    kpe = jnp.concatenate([r[0] for r in kpe_refs], axis=0)
    process(c, kpe, s_ref[0], None)

    @pl.when(g == pl.num_programs(1) - 1)
    def _():
        o_ref[0] = acc_sc[...] / l_sc[...]


def paged_attention(q_lat, q_r, c_new, kpe_new, s_new, cache_c, cache_kpe, s_past, page_table_flat,
                    n_heads, t_len, scale, npg):
    bsz, rows, d_c = q_lat.shape
    d_r = q_r.shape[2]
    page = cache_c.shape[1]
    n_pages_b = page_table_flat.shape[0] // bsz
    n_grp = n_pages_b // npg
    kn = c_new.shape[1]

    def page_spec(width, p):
        return pl.BlockSpec((1, page, width),
                            lambda b, g, pt, p=p: (pt[b * n_pages_b + g * npg + p], 0, 0))

    def per_b(shape):
        return pl.BlockSpec((1,) + shape, lambda b, g, pt: (b, 0, 0))

    in_specs = ([per_b((rows, d_c)), per_b((rows, d_r)), per_b((kn, d_c)), per_b((kn, d_r)),
                 per_b((n_heads, kn))]
                + [page_spec(d_c, p) for p in range(npg)] + [page_spec(d_r, p) for p in range(npg)]
                + [pl.BlockSpec((1, n_heads, npg * page), lambda b, g, pt: (b * n_grp + g, 0, 0))])
    grid_spec = pltpu.PrefetchScalarGridSpec(
        num_scalar_prefetch=1, grid=(bsz, n_grp), in_specs=in_specs,
        out_specs=per_b((rows, d_c)),
        scratch_shapes=[pltpu.VMEM((rows, 1), F32), pltpu.VMEM((rows, 1), F32),
                        pltpu.VMEM((rows, d_c), F32)])
    return pl.pallas_call(
        functools.partial(_paged_attn_kernel, npg=npg, n_heads=n_heads, t_len=t_len, scale=scale),
        grid_spec=grid_spec,
        out_shape=jax.ShapeDtypeStruct((bsz, rows, d_c), F32),
        compiler_params=_cparams(("parallel", "arbitrary")),
        name="paged_attention",
    )(page_table_flat, q_lat, q_r, c_new, kpe_new, s_new,
      *([cache_c] * npg), *([cache_kpe] * npg), s_past)


def _rms_norm(x, g):
    return x * lax.rsqrt(jnp.mean(x * x, axis=-1, keepdims=True) + RMS_EPS) * g


def _rope(x, pos):
    half = x.shape[-1] // 2
    inv = ROPE_BASE ** (-jnp.arange(half, dtype=F32) / half)
    ang = pos.astype(F32)[:, None] * inv[None, :]
    cos = jnp.cos(ang)[None, :, None, :]
    sin = jnp.sin(ang)[None, :, None, :]
    x1, x2 = x[..., :half], x[..., half:]
    return jnp.concatenate([x1 * cos - x2 * sin, x1 * sin + x2 * cos], axis=-1)


def _conv_ffn(x, prev, p):
    h = _rms_norm(x, p["norm"])
    u = mm(h, p["w_gate"])
    up = mm(h, p["w_up"])
    t = x.shape[1]
    ext = jnp.concatenate([prev, u], axis=1)
    conv = p["conv_b"]
    for j in range(CONV_W):
        conv = conv + p["conv_w"][j] * ext[:, j:j + t]
    hid = (jax.nn.silu(conv) * up).astype(BF16)
    return x + mm(hid, p["w_down"]), ext[:, -(CONV_W - 1):]


def _rwkv7_time_mix(x, shift_prev, wkv_prev, v_first, p, lanes_are_batch):
    bsz, t, d = x.shape
    n_heads, n = p["r_k"].shape
    h = _rms_norm(x, p["norm"])
    prev = jnp.concatenate([shift_prev[:, None, :], h[:, :-1]], axis=1)
    xx = prev - h
    xr, xw, xk, xv, xa, xg = (h + xx * p["mix"][i] for i in range(6))
    r = mm(xr, p["wr"])
    k = mm(xk, p["wk"])
    v = mm(xv, p["wv"])
    w_log = -jax.nn.softplus(-(p["w0"] + mm(jnp.tanh(mm(xw, p["w1"])), p["w2"]))) - 0.5
    decay = jnp.exp(-jnp.exp(w_log))
    a = jax.nn.sigmoid(p["a0"] + mm(mm(xa, p["a1"]), p["a2"]))
    g = mm(jax.nn.sigmoid(mm(xg, p["g1"])), p["g2"])
    if p["v_lora"] is None:
        v_first = v
    else:
        v0, v1, v2 = p["v_lora"]
        v = v + (v_first - v) * jax.nn.sigmoid(v0 + mm(mm(xv, v1), v2))

    def hd(z):
        return z.reshape(bsz, t, n_heads, n)

    kk = hd(k * p["k_k"])
    kk = kk / jnp.maximum(jnp.sqrt(jnp.sum(kk * kk, axis=-1, keepdims=True)), 1e-12)
    k = k * (1 + (a - 1) * p["k_a"])
    rh, kh, vh, ah = hd(r), hd(k), hd(v), hd(a)

    if lanes_are_batch:
        to_k = lambda z: z.transpose(2, 1, 3, 0)
        s0 = wkv_prev.transpose(1, 2, 3, 0)
    else:
        to_k = lambda z: z.transpose(1, 3, 0, 2).reshape(1, t, n, bsz * n_heads)
        s0 = wkv_prev.transpose(2, 3, 0, 1).reshape(1, n, n, bsz * n_heads)
    y, s = wkv_scan(to_k(rh), to_k(hd(decay)), to_k(kh), to_k(vh), to_k(-kk), to_k(kk * ah), s0)
    if lanes_are_batch:
        y = y.transpose(3, 1, 0, 2)
        s = s.transpose(3, 0, 1, 2)
    else:
        y = y.reshape(t, n, bsz, n_heads).transpose(2, 0, 3, 1)
        s = s.reshape(n, n, bsz, n_heads).transpose(2, 3, 0, 1)

    mu = jnp.mean(y, axis=-1, keepdims=True)
    var = jnp.mean((y - mu) ** 2, axis=-1, keepdims=True)
    y = ((y - mu) * lax.rsqrt(var + n * GN_EPS_PER_CHANNEL)).reshape(bsz, t, d) * p["ln_w"] + p["ln_b"]
    y = y + (jnp.sum(rh * kh * p["r_k"], axis=-1, keepdims=True) * vh).reshape(bsz, t, d)
    out = mm(y * g, p["wo"])
    return x + out, h[:, -1], s, v_first


def _mla_kv_side(x, pos, kv):
    u = _rms_norm(x, kv["norm"])
    d_c = kv["norm_c"].shape[0]
    ckr = mm(u, kv["w_dkv_kr"])
    c = _rms_norm(ckr[..., :d_c], kv["norm_c"])
    k_pe = _rope(ckr[..., d_c:][:, :, None, :], pos)[:, :, 0, :]
    return c, k_pe


def _mla_query(x, pos, p, n_heads, d_nope):
    bsz, t, _ = x.shape
    h = _rms_norm(x, p["norm"])
    cq = _rms_norm(mm(h, p["w_dq"]), p["norm_cq"])
    q = mm(cq, p["w_uq"]).reshape(bsz, t, n_heads, -1)
    q = jnp.concatenate([q[..., :d_nope], _rope(q[..., d_nope:], pos)], axis=-1)
    return _rms_norm(q, p["norm_q"])


def _prompt_attend_fn(c, k_pe, kv, n_heads, d_nope):
    bsz, s_len, _ = c.shape
    k_nope = mm(c, kv["w_uk_2d"]).reshape(bsz, s_len, n_heads, d_nope)
    k = jnp.concatenate(
        [k_nope, jnp.broadcast_to(k_pe[:, :, None, :], (bsz, s_len, n_heads, k_pe.shape[-1]))], axis=-1)
    k = _rms_norm(k, kv["norm_k"]).astype(BF16).transpose(0, 2, 1, 3)
    v = mm(c, kv["w_uv_2d"], BF16)
    scale = k.shape[-1] ** -0.5

    def attend(q):
        return flash_prompt(q.astype(BF16).transpose(0, 2, 1, 3), k, v, n_heads, scale)

    return attend


def _sample_attend_fn(c_new, kpe_new, cache_c, cache_kpe, page_table, kv, n_heads, d_nope):
    bsz, t, d_c = c_new.shape
    d_r = kpe_new.shape[-1]
    d_qk = d_nope + d_r
    page = cache_c.shape[1]
    pt_flat = page_table.reshape(-1)
    npg = _pick(page_table.shape[1], (PAGES_PER_STEP, 4, 2, 1))

    k_nope_new = mm(c_new, kv["w_uk_2d"]).reshape(bsz, t, n_heads, d_nope)
    ss = jnp.sum(k_nope_new * k_nope_new, axis=-1) + jnp.sum(kpe_new * kpe_new, axis=-1)[..., None]
    s_new = lax.rsqrt(ss / d_qk + RMS_EPS)
    kn = page
    pad = lambda z: jnp.pad(z, ((0, 0), (0, kn - t), (0, 0)))
    c_new_p, kpe_new_p = pad(c_new), pad(kpe_new)
    s_new_p = jnp.pad(s_new.transpose(0, 2, 1), ((0, 0), (0, 0), (0, kn - t)), constant_values=1.0)

    s_past = paged_key_scale(cache_c, cache_kpe, pt_flat, kv["w_uk_t"], n_heads, d_qk, npg)
    g_kn, g_kr = kv["norm_k"][:d_nope], kv["norm_k"][d_nope:]
    scale = d_qk ** -0.5

    def attend(q):
        qn = (q[..., :d_nope] * g_kn).transpose(2, 0, 1, 3).reshape(n_heads, bsz * t, d_nope)
        q_lat = bmm(qn, kv["w_uk_hdc"], BF16)
        q_lat = q_lat.reshape(n_heads, bsz, t, d_c).transpose(1, 0, 2, 3).reshape(bsz, n_heads * t, d_c)
        q_r = (q[..., d_nope:] * g_kr).astype(BF16).transpose(0, 2, 1, 3).reshape(bsz, n_heads * t, d_r)
        o_lat = paged_attention(q_lat, q_r, c_new_p, kpe_new_p, s_new_p, cache_c, cache_kpe, s_past,
                                pt_flat, n_heads, t, scale, npg)
        o_lat = o_lat.reshape(bsz, n_heads, t, d_c).transpose(1, 0, 2, 3).reshape(n_heads, bsz * t, d_c)
        o = bmm(o_lat, kv["w_uv_hcd"])
        return o.reshape(n_heads, bsz, t, -1).transpose(1, 2, 0, 3).reshape(bsz, t, -1)

    return attend


def kernel(x_prompt, x_sample, cache_kv_latent, cache_k_pe, state_wkv, state_shift, state_conv, page_table,
           ffn_norm, ffn_w_gate, ffn_w_up, ffn_conv_w, ffn_conv_b, ffn_w_down,
           rw_norm, rw_mix, rw_w0, rw_w1, rw_w2, rw_a0, rw_a1, rw_a2, rw_v0, rw_v1, rw_v2,
           rw_g1, rw_g2, rw_k_k, rw_k_a, rw_r_k, rw_wr, rw_wk, rw_wv, rw_ln_w, rw_ln_b, rw_wo,
           kv_norm, kv_w_dkv, kv_norm_c, kv_w_kr, kv_w_uk, kv_w_uv, kv_norm_k,
           mla_norm, mla_w_dq, mla_norm_cq, mla_w_uq, mla_norm_q, mla_wo):
    depth = ffn_norm.shape[0]
    n_a = rw_norm.shape[0]
    d_c, n_heads, d_nope = kv_w_uk.shape
    d_v = kv_w_uv.shape[2]
    bf = lambda z: z.astype(BF16)

    ffn = [dict(norm=ffn_norm[l], w_gate=bf(ffn_w_gate[l]), w_up=bf(ffn_w_up[l]), conv_w=ffn_conv_w[l],
                conv_b=ffn_conv_b[l], w_down=bf(ffn_w_down[l])) for l in range(depth)]
    rw = [dict(norm=rw_norm[l], mix=rw_mix[l], w0=rw_w0[l], w1=bf(rw_w1[l]), w2=bf(rw_w2[l]),
               a0=rw_a0[l], a1=bf(rw_a1[l]), a2=bf(rw_a2[l]), g1=bf(rw_g1[l]), g2=bf(rw_g2[l]),
               k_k=rw_k_k[l], k_a=rw_k_a[l], r_k=rw_r_k[l], wr=bf(rw_wr[l]), wk=bf(rw_wk[l]),
               wv=bf(rw_wv[l]), ln_w=rw_ln_w[l], ln_b=rw_ln_b[l], wo=bf(rw_wo[l]),
               v_lora=None if l == 0 else (rw_v0[l - 1], bf(rw_v1[l - 1]), bf(rw_v2[l - 1])))
          for l in range(n_a)]
    kv = dict(norm=kv_norm, norm_c=kv_norm_c, norm_k=kv_norm_k,
              w_dkv_kr=bf(jnp.concatenate([kv_w_dkv, kv_w_kr], axis=1)),
              w_uk_2d=bf(kv_w_uk.reshape(d_c, n_heads * d_nope)),
              w_uv_2d=bf(kv_w_uv.reshape(d_c, n_heads * d_v)),
              w_uk_t=bf(kv_w_uk.reshape(d_c, n_heads * d_nope).T),
              w_uk_hdc=bf(kv_w_uk.transpose(1, 2, 0)),
              w_uv_hcd=bf(kv_w_uv.transpose(1, 0, 2)))
    mla = [dict(norm=mla_norm[b], w_dq=bf(mla_w_dq[b]), norm_cq=mla_norm_cq[b],
                w_uq=bf(mla_w_uq[b].reshape(mla_w_uq.shape[1], -1)), norm_q=mla_norm_q[b],
                wo=bf(mla_wo[b])) for b in range(depth - n_a)]

    def trunk(x, pos, shift_in, wkv_in, conv_in, make_attend, lanes_are_batch):
        v_first = attend = c = k_pe = None
        shifts, wkvs, convs = [], [], []
        for layer in range(depth):
            if layer < n_a:
                x, sh, s, v_first = _rwkv7_time_mix(x, shift_in[layer], wkv_in[layer], v_first, rw[layer],
                                                    lanes_are_batch)
                shifts.append(sh)
                wkvs.append(s)
            else:
                if layer == n_a:
                    c, k_pe = _mla_kv_side(x, pos, kv)
                    attend = make_attend(c, k_pe)
                p = mla[layer - n_a]
                q = _mla_query(x, pos, p, n_heads, d_nope)
                x = x + mm(attend(q), p["wo"])
            x, cv = _conv_ffn(x, conv_in[layer], ffn[layer])
            convs.append(cv)
        return x, c, k_pe, jnp.stack(wkvs), jnp.stack(shifts), jnp.stack(convs)

    bp, sp, d = x_prompt.shape
    n_rw_heads, n = rw_r_k.shape[1:]
    d_ff = ffn_w_gate.shape[2]
    out_p = trunk(
        x_prompt, jnp.arange(sp),
        jnp.zeros((n_a, bp, d), F32), jnp.zeros((n_a, bp, n_rw_heads, n, n), F32),
        jnp.zeros((depth, bp, CONV_W - 1, d_ff), F32),
        lambda c, kp: _prompt_attend_fn(c, kp, kv, n_heads, d_nope), False)

    past_len = page_table.shape[1] * cache_kv_latent.shape[1]
    out_s = trunk(
        x_sample, past_len + jnp.arange(x_sample.shape[1]), state_shift, state_wkv, state_conv,
        lambda c, kp: _sample_attend_fn(c, kp, cache_kv_latent, cache_k_pe, page_table, kv, n_heads, d_nope),
        True)

    return (out_p[0], out_s[0]) + tuple(out_p[1:]) + tuple(out_s[1:])
```

```python
import functools

import jax
import jax.numpy as jnp
from jax import lax
from jax.experimental import pallas as pl
from jax.experimental.pallas import tpu as pltpu

F32 = jnp.float32
BF16 = jnp.bfloat16

RMS_EPS = 1e-6
ROPE_BASE = 10000.0
GN_EPS_PER_CHANNEL = 1e-5
CONV_W = 3

VMEM_LIMIT_BYTES = 52 * 1024 * 1024
LANES = 128
PAGES_PER_STEP = 8
ROW_BLOCK = 1024
_ROW_BLOCKS = (1024, 512, 256, 128, 64, 32, 16, 8)


def _pick(n, candidates):
    for c in candidates:
        if c <= n and n % c == 0:
            return c
    return n


def _cparams(sem):
    return pltpu.CompilerParams(dimension_semantics=sem, vmem_limit_bytes=VMEM_LIMIT_BYTES)


def _mm_kernel(x_ref, w_ref, o_ref):
    o_ref[0] = jnp.dot(x_ref[0].astype(BF16), w_ref[0],
                       preferred_element_type=F32).astype(o_ref.dtype)


def _mm_res_kernel(x_ref, w_ref, r_ref, o_ref):
    o_ref[0] = r_ref[0] + jnp.dot(x_ref[0].astype(BF16), w_ref[0], preferred_element_type=F32)


def bmm(x, w, out_dtype=F32, res=None):
    g, m, k = x.shape
    n = w.shape[2]
    bm = _pick(m, (1024, 512, 256, 128, 64, 32, 16, 8))
    bn = n if n <= 1024 else _pick(n, (512, 256, 128))
    out_spec = pl.BlockSpec((1, bm, bn), lambda a, i, j: (a, i, j))
    in_specs = [pl.BlockSpec((1, bm, k), lambda a, i, j: (a, i, 0)),
                pl.BlockSpec((1, k, bn), lambda a, i, j: (a, 0, j))]
    return pl.pallas_call(
        _mm_kernel if res is None else _mm_res_kernel,
        grid=(g, m // bm, n // bn),
        in_specs=in_specs if res is None else in_specs + [out_spec],
        out_specs=out_spec,
        out_shape=jax.ShapeDtypeStruct((g, m, n), out_dtype),
        compiler_params=_cparams(("parallel", "parallel", "arbitrary")),
        name="bmm",
    )(*((x, w) if res is None else (x, w, res)))


def mm(x, w, out_dtype=F32, res=None):
    lead = x.shape[:-1]
    n = w.shape[1]
    out = bmm(x.reshape((1, -1, x.shape[-1])), w[None], out_dtype,
              None if res is None else res.reshape((1, -1, n)))
    return out.reshape(lead + (n,))


def _ffn_in_kernel(*refs, t_len, bm, seq_in_block):
    if seq_in_block:
        x_ref, wg_ref, wu_ref, cw_ref, cb_ref, e_ref, hid_ref, tail_ref = refs
    else:
        x_ref, xp_ref, wg_ref, wu_ref, cw_ref, cb_ref, e_ref, hid_ref, tail_ref = refs
    x = x_ref[...]
    wg = wg_ref[...]
    u = jnp.dot(x, wg, preferred_element_type=F32)
    up = jnp.dot(x, wu_ref[...], preferred_element_type=F32)
    row = lax.broadcasted_iota(jnp.int32, u.shape, 0)
    if seq_in_block:
        t = row % t_len
        u1 = jnp.where(t < 1, e_ref[0], pltpu.roll(u, 1, axis=0))
        u2 = jnp.where(t < 2, e_ref[1], pltpu.roll(u, 2, axis=0))
        tail_ref[...] = u
    else:
        prev8 = jnp.dot(xp_ref[...], wg, preferred_element_type=F32)
        seq_start = (pl.program_id(0) * bm) % t_len == 0
        prev8 = jnp.where(seq_start, e_ref[0], prev8)
        u1 = jnp.where(row < 1, prev8[7:8], pltpu.roll(u, 1, axis=0))
        u2 = jnp.where(row < 1, prev8[6:7], jnp.where(row < 2, prev8[7:8], pltpu.roll(u, 2, axis=0)))
        tail_ref[0] = u[bm - 8:bm]
    conv = cb_ref[...] + cw_ref[0:1] * u2 + cw_ref[1:2] * u1 + cw_ref[2:3] * u
    hid_ref[...] = (conv * jax.nn.sigmoid(conv) * up).astype(hid_ref.dtype)


def ffn_in(h, prev, w_gate, w_up, conv_w, conv_b):
    bsz, t_len, d = h.shape
    f = w_gate.shape[1]
    m = bsz * t_len
    x = h.reshape(m, d)
    bn = _pick(f, (512, 256, 128))
    seq_in_block = t_len < ROW_BLOCK
    if seq_in_block:
        bm = _pick(m, tuple(c for c in _ROW_BLOCKS if c <= ROW_BLOCK and c % t_len == 0))
        tpos = jnp.arange(t_len)[None, :, None]
        e = jnp.stack([jnp.broadcast_to(prev[:, 1:2], (bsz, t_len, f)),
                       jnp.where(tpos == 0, prev[:, 0:1], prev[:, 1:2])]).reshape(2, m, f)
        lead = []
        lead_specs = []
        e_spec = pl.BlockSpec((2, bm, bn), lambda i, j: (0, i, j))
        tail_shape = jax.ShapeDtypeStruct((m, f), F32)
        tail_spec = pl.BlockSpec((bm, bn), lambda i, j: (i, j))
    else:
        bm = _pick(t_len, tuple(c for c in _ROW_BLOCKS if c <= ROW_BLOCK))
        e = jnp.pad(prev, ((0, 0), (6, 0), (0, 0)))
        lead = [x]
        lead_specs = [pl.BlockSpec((8, d), lambda i, j: (jnp.maximum(i * (bm // 8) - 1, 0), 0))]
        e_spec = pl.BlockSpec((1, 8, bn), lambda i, j: ((i * bm) // t_len, 0, j))
        tail_shape = jax.ShapeDtypeStruct((m // bm, 8, f), F32)
        tail_spec = pl.BlockSpec((1, 8, bn), lambda i, j: (i, 0, j))
    w_spec = pl.BlockSpec((d, bn), lambda i, j: (0, j))
    hid, tail = pl.pallas_call(
        functools.partial(_ffn_in_kernel, t_len=t_len, bm=bm, seq_in_block=seq_in_block),
        grid=(m // bm, f // bn),
        in_specs=([pl.BlockSpec((bm, d), lambda i, j: (i, 0))] + lead_specs
                  + [w_spec, w_spec, pl.BlockSpec((CONV_W, bn), lambda i, j: (0, j)),
                     pl.BlockSpec((1, bn), lambda i, j: (0, j)), e_spec]),
        out_specs=[pl.BlockSpec((bm, bn), lambda i, j: (i, j)), tail_spec],
        out_shape=[jax.ShapeDtypeStruct((m, f), BF16), tail_shape],
        compiler_params=_cparams(("parallel", "arbitrary")),
        name="ffn_in",
    )(x, *lead, w_gate, w_up, conv_w, conv_b.reshape(1, f), e)
    if seq_in_block:
        state = tail.reshape(bsz, t_len, f)[:, t_len - 2:]
    else:
        state = tail.reshape(bsz, t_len // bm, 8, f)[:, -1, 6:]
    return hid, state


def _wkv_kernel(r_ref, w_ref, k_ref, v_ref, a_ref, b_ref, s0_ref, y_ref, s_ref, *, tc, n, ig):
    @pl.when(pl.program_id(1) == 0)
    def _():
        s_ref[...] = s0_ref[...]

    for grp in range(n // ig):
        i0 = grp * ig

        def step(t, carry, i0=i0):
            a_t = a_ref[0, t]
            w_t = w_ref[0, t]
            b_t = b_ref[0, t]
            k_t = k_ref[0, t]
            r_t = r_ref[0, t]
            v_t = v_ref[0, t, i0:i0 + ig, :]
            new, ys = [], []
            for ii in range(ig):
                s = carry[ii]
                sa = jnp.sum(s * a_t, axis=0, keepdims=True)
                s = s * w_t + sa * b_t + v_t[ii:ii + 1] * k_t
                ys.append(jnp.sum(s * r_t, axis=0, keepdims=True))
                new.append(s)
            y_ref[0, t, i0:i0 + ig, :] = jnp.concatenate(ys, axis=0)
            return tuple(new)

        init = tuple(s_ref[0, i0 + ii] for ii in range(ig))
        out = lax.fori_loop(0, tc, step, init)
        for ii in range(ig):
            s_ref[0, i0 + ii] = out[ii]


def wkv_scan(r, w, k, v, a, b, s0):
    g, t, n, l = r.shape
    tc = _pick(t, (64, 32, 16, 8))
    seq_spec = pl.BlockSpec((1, tc, n, l), lambda a_, c: (a_, c, 0, 0))
    st_spec = pl.BlockSpec((1, n, n, l), lambda a_, c: (a_, 0, 0, 0))
    return pl.pallas_call(
        functools.partial(_wkv_kernel, tc=tc, n=n, ig=4),
        grid=(g, t // tc),
        in_specs=[seq_spec] * 6 + [st_spec],
        out_specs=[seq_spec, st_spec],
        out_shape=[jax.ShapeDtypeStruct((g, t, n, l), F32),
                   jax.ShapeDtypeStruct((g, n, n, l), F32)],
        compiler_params=_cparams(("parallel", "arbitrary")),
        name="wkv_scan",
    )(r, w, k, v, a, b, s0)


def _flash_kernel(q_ref, k_ref, v_ref, o_ref, *, bq, n_q, scale):
    qi = pl.program_id(2)
    q = q_ref[0]
    dn = (((1,), (1,)), ((), ()))

    for i in range(n_q):
        @pl.when(qi == i)
        def _(i=i):
            kv_len = (i + 1) * bq
            s = lax.dot_general(q, k_ref[0, :kv_len, :], dn, preferred_element_type=F32) * scale
            qpos = i * bq + lax.broadcasted_iota(jnp.int32, (bq, kv_len), 0)
            kpos = lax.broadcasted_iota(jnp.int32, (bq, kv_len), 1)
            s = jnp.where(kpos <= qpos, s, -jnp.inf)
            p = jnp.exp(s - jnp.max(s, axis=-1, keepdims=True))
            l = jnp.sum(p, axis=-1, keepdims=True)
            o = jnp.dot(p.astype(BF16), v_ref[0, :kv_len, :], preferred_element_type=F32)
            o_ref[0] = (o / l).astype(o_ref.dtype)


def flash_prompt(q, k, v, n_heads, scale):
    bsz, s_len, _ = q.shape
    dp = q.shape[-1] // n_heads
    dv = v.shape[-1] // n_heads
    bq = _pick(s_len, (512, 256, 128, 64, 32, 16, 8))
    n_q = s_len // bq
    return pl.pallas_call(
        functools.partial(_flash_kernel, bq=bq, n_q=n_q, scale=scale),
        grid=(bsz, n_heads, n_q),
        in_specs=[pl.BlockSpec((1, bq, dp), lambda b, h, i: (b, i, h)),
                  pl.BlockSpec((1, s_len, dp), lambda b, h, i: (b, 0, h)),
                  pl.BlockSpec((1, s_len, dv), lambda b, h, i: (b, 0, h))],
        out_specs=pl.BlockSpec((1, bq, dv), lambda b, h, i: (b, i, h)),
        out_shape=jax.ShapeDtypeStruct((bsz, s_len, n_heads * dv), BF16),
        compiler_params=_cparams(("parallel", "parallel", "arbitrary")),
        name="flash_prompt",
    )(q, k, v)


def _sum_rows_to_lanes(x):
    hi = x.astype(BF16)
    lo = (x - hi.astype(F32)).astype(BF16)
    ones = jnp.ones((8, x.shape[1]), BF16)
    dn = (((1,), (1,)), ((), ()))
    out = (lax.dot_general(ones, hi, dn, preferred_element_type=F32)
           + lax.dot_general(ones, lo, dn, preferred_element_type=F32))
    return out[0:1]


def _key_scale_block(wt, c, kpe, n_heads, d_qk):
    k_t = lax.dot_general(wt, c.astype(BF16), (((1,), (1,)), ((), ())), preferred_element_type=F32)
    sq = k_t * k_t
    ss = jnp.sum(sq.reshape(n_heads, sq.shape[0] // n_heads, sq.shape[1]), axis=1)
    kpe = kpe.astype(F32)
    pe = _sum_rows_to_lanes(kpe * kpe)
    return lax.rsqrt((ss + pe) / d_qk + RMS_EPS)


def _kscale_kernel(pt_ref, *refs, npg, n_heads, d_qk):
    del pt_ref
    c_refs = refs[:npg]
    kpe_refs = refs[npg:2 * npg]
    wt_ref = refs[2 * npg]
    s_ref, ckv_ref = refs[2 * npg + 1:]
    page, d_c = c_refs[0].shape[1:]
    d_r = kpe_refs[0].shape[2]
    pair = 2 if npg % 2 == 0 else 1
    for p in range(0, npg, pair):
        rows = slice(p * page, (p + pair) * page)
        c = jnp.concatenate([c_refs[p + q][0] for q in range(pair)], axis=0)
        kpe = jnp.concatenate([kpe_refs[p + q][0] for q in range(pair)], axis=0)
        s_ref[0, :, rows] = _key_scale_block(wt_ref[...], c, kpe, n_heads, d_qk)
        ckv_ref[0, rows, 0:d_c] = c.astype(BF16)
        ckv_ref[0, rows, d_c:d_c + d_r] = kpe.astype(BF16)


def paged_key_scale(cache_c, cache_kpe, page_table_flat, w_uk_t, n_heads, d_qk, npg):
    n_used = page_table_flat.shape[0]
    page, d_c = cache_c.shape[1:]
    d_r = cache_kpe.shape[2]
    n_grp = n_used // npg

    def page_spec(width, p):
        return pl.BlockSpec((1, page, width), lambda g, pt, p=p: (pt[g * npg + p], 0, 0))

    in_specs = ([page_spec(d_c, p) for p in range(npg)] + [page_spec(d_r, p) for p in range(npg)]
                + [pl.BlockSpec(w_uk_t.shape, lambda g, pt: (0, 0))])
    grid_spec = pltpu.PrefetchScalarGridSpec(
        num_scalar_prefetch=1, grid=(n_grp,), in_specs=in_specs,
        out_specs=[pl.BlockSpec((1, n_heads, npg * page), lambda g, pt: (g, 0, 0)),
                   pl.BlockSpec((1, npg * page, d_c + d_r), lambda g, pt: (g, 0, 0))])
    return pl.pallas_call(
        functools.partial(_kscale_kernel, npg=npg, n_heads=n_heads, d_qk=d_qk),
        grid_spec=grid_spec,
        out_shape=[jax.ShapeDtypeStruct((n_grp, n_heads, npg * page), F32),
                   jax.ShapeDtypeStruct((n_grp, npg * page, d_c + d_r), BF16)],
        compiler_params=_cparams(("arbitrary",)),
        name="paged_key_scale",
    )(page_table_flat, *([cache_c] * npg), *([cache_kpe] * npg), w_uk_t)


def _paged_attn_kernel(q_ref, new_ref, snew_ref, ckv_ref, s_ref, o_ref, m_sc, l_sc, acc_sc, *,
                       bb, n_heads, t_len, d_c, scale):
    g = pl.program_id(1)
    rows = n_heads * t_len
    dn = (((1,), (1,)), ((), ()))

    def process(b, ckv, key_scale, mask):
        sc = lax.dot_general(q_ref[b], ckv, dn, preferred_element_type=F32)
        row_scale = jnp.concatenate([key_scale] * t_len, axis=0)
        sc = sc * (row_scale * scale)
        if mask is not None:
            sc = jnp.where(mask, sc, -jnp.inf)
        m = m_sc[b]
        m_new = jnp.maximum(m, jnp.max(sc, axis=-1, keepdims=True))
        alpha = jnp.exp(m - m_new)
        p = jnp.exp(sc - m_new)
        l_sc[b] = l_sc[b] * alpha + jnp.sum(p, axis=-1, keepdims=True)
        acc_sc[b] = acc_sc[b] * alpha + jnp.dot(p.astype(BF16), ckv[:, :d_c],
                                                preferred_element_type=F32)
        m_sc[b] = m_new

    @pl.when(g == 0)
    def _():
        m_sc[...] = jnp.full(m_sc.shape, -jnp.inf, F32)
        l_sc[...] = jnp.zeros(l_sc.shape, F32)
        acc_sc[...] = jnp.zeros(acc_sc.shape, F32)
        nk = new_ref.shape[1]
        t_row = lax.broadcasted_iota(jnp.int32, (rows, nk), 0) // n_heads
        key = lax.broadcasted_iota(jnp.int32, (rows, nk), 1)
        for b in range(bb):
            process(b, new_ref[b], snew_ref[b], key <= t_row)

    for b in range(bb):
        process(b, ckv_ref[b, 0], s_ref[b, 0], None)

    @pl.when(g == pl.num_programs(1) - 1)
    def _():
        for b in range(bb):
            o_ref[b] = acc_sc[b] / l_sc[b]


def paged_attention(q_cat, ckv_new, s_new, ckv, s_past, n_heads, t_len, d_c, scale):
    bsz, rows, d_cat = q_cat.shape
    _, n_grp, nk, _ = ckv.shape
    kn = ckv_new.shape[1]
    bb = _pick(bsz, (2, 1))

    def per_b(shape):
        return pl.BlockSpec((bb,) + shape, lambda b, g: (b, 0, 0))

    return pl.pallas_call(
        functools.partial(_paged_attn_kernel, bb=bb, n_heads=n_heads, t_len=t_len, d_c=d_c, scale=scale),
        grid=(bsz // bb, n_grp),
        in_specs=[per_b((rows, d_cat)), per_b((kn, d_cat)), per_b((n_heads, kn)),
                  pl.BlockSpec((bb, 1, nk, d_cat), lambda b, g: (b, g, 0, 0)),
                  pl.BlockSpec((bb, 1, n_heads, nk), lambda b, g: (b, g, 0, 0))],
        out_specs=per_b((rows, d_c)),
        out_shape=jax.ShapeDtypeStruct((bsz, rows, d_c), F32),
        scratch_shapes=[pltpu.VMEM((bb, rows, 1), F32), pltpu.VMEM((bb, rows, 1), F32),
                        pltpu.VMEM((bb, rows, d_c), F32)],
        compiler_params=_cparams(("parallel", "arbitrary")),
        name="paged_attention",
    )(q_cat, ckv_new, s_new, ckv, s_past)


def _rms_norm(x, g):
    return x * lax.rsqrt(jnp.mean(x * x, axis=-1, keepdims=True) + RMS_EPS) * g


def _rope(x, pos):
    half = x.shape[-1] // 2
    inv = ROPE_BASE ** (-jnp.arange(half, dtype=F32) / half)
    ang = pos.astype(F32)[:, None] * inv[None, :]
    cos = jnp.cos(ang)[None, :, None, :]
    sin = jnp.sin(ang)[None, :, None, :]
    x1, x2 = x[..., :half], x[..., half:]
    return jnp.concatenate([x1 * cos - x2 * sin, x1 * sin + x2 * cos], axis=-1)


def _conv_ffn(x, prev, p):
    h = _rms_norm(x, p["norm"]).astype(BF16)
    hid, state = ffn_in(h, prev, p["w_gate"], p["w_up"], p["conv_w"], p["conv_b"])
    return mm(hid, p["w_down"], res=x.reshape(hid.shape[0], -1)).reshape(x.shape), state


def _rwkv7_time_mix(x, shift_prev, wkv_prev, v_first, p, lanes_are_batch):
    bsz, t, d = x.shape
    n_heads, n = p["r_k"].shape
    h = _rms_norm(x, p["norm"])
    prev = jnp.concatenate([shift_prev[:, None, :], h[:, :-1]], axis=1)
    xx = prev - h
    xr, xw, xk, xv, xa, xg = ((h + xx * p["mix"][i]).astype(BF16) for i in range(6))
    r = mm(xr, p["wr"])
    k = mm(xk, p["wk"])
    v = mm(xv, p["wv"])
    w_log = -jax.nn.softplus(-(p["w0"] + mm(jnp.tanh(mm(xw, p["w1"])), p["w2"]))) - 0.5
    decay = jnp.exp(-jnp.exp(w_log))
    a = jax.nn.sigmoid(p["a0"] + mm(mm(xa, p["a1"]), p["a2"]))
    g = mm(jax.nn.sigmoid(mm(xg, p["g1"])), p["g2"])
    if p["v_lora"] is None:
        v_first = v
    else:
        v0, v1, v2 = p["v_lora"]
        v = v + (v_first - v) * jax.nn.sigmoid(v0 + mm(mm(xv, v1), v2))

    def hd(z):
        return z.reshape(bsz, t, n_heads, n)

    kk = hd(k * p["k_k"])
    kk = kk / jnp.maximum(jnp.sqrt(jnp.sum(kk * kk, axis=-1, keepdims=True)), 1e-12)
    k = k * (1 + (a - 1) * p["k_a"])
    rh, kh, vh, ah = hd(r), hd(k), hd(v), hd(a)

    if lanes_are_batch:
        to_k = lambda z: z.transpose(2, 1, 3, 0)
        s0 = wkv_prev.transpose(1, 2, 3, 0)
    else:
        to_k = lambda z: z.transpose(1, 3, 0, 2).reshape(1, t, n, bsz * n_heads)
        s0 = wkv_prev.transpose(2, 3, 0, 1).reshape(1, n, n, bsz * n_heads)
    y, s = wkv_scan(to_k(rh), to_k(hd(decay)), to_k(kh), to_k(vh), to_k(-kk), to_k(kk * ah), s0)
    if lanes_are_batch:
        y = y.transpose(3, 1, 0, 2)
        s = s.transpose(3, 0, 1, 2)
    else:
        y = y.reshape(t, n, bsz, n_heads).transpose(2, 0, 3, 1)
        s = s.reshape(n, n, bsz, n_heads).transpose(2, 3, 0, 1)

    mu = jnp.mean(y, axis=-1, keepdims=True)
    var = jnp.mean((y - mu) ** 2, axis=-1, keepdims=True)
    y = ((y - mu) * lax.rsqrt(var + n * GN_EPS_PER_CHANNEL)).reshape(bsz, t, d) * p["ln_w"] + p["ln_b"]
    y = y + (jnp.sum(rh * kh * p["r_k"], axis=-1, keepdims=True) * vh).reshape(bsz, t, d)
    return mm((y * g).astype(BF16), p["wo"], res=x), h[:, -1], s, v_first


def _mla_kv_side(x, pos, kv):
    u = _rms_norm(x, kv["norm"]).astype(BF16)
    d_c = kv["norm_c"].shape[0]
    ckr = mm(u, kv["w_dkv_kr"])
    c = _rms_norm(ckr[..., :d_c], kv["norm_c"])
    k_pe = _rope(ckr[..., d_c:][:, :, None, :], pos)[:, :, 0, :]
    return c, k_pe


def _mla_query(x, pos, p, n_heads, d_nope):
    bsz, t, _ = x.shape
    h = _rms_norm(x, p["norm"]).astype(BF16)
    cq = _rms_norm(mm(h, p["w_dq"]), p["norm_cq"]).astype(BF16)
    q = mm(cq, p["w_uq"]).reshape(bsz, t, n_heads, -1)
    q = jnp.concatenate([q[..., :d_nope], _rope(q[..., d_nope:], pos)], axis=-1)
    return _rms_norm(q, p["norm_q"])


def _prompt_attend_fn(c, k_pe, kv, n_heads, d_nope):
    bsz, s_len, _ = c.shape
    k_nope = mm(c, kv["w_uk_2d"]).reshape(bsz, s_len, n_heads, d_nope)
    k = jnp.concatenate(
        [k_nope, jnp.broadcast_to(k_pe[:, :, None, :], (bsz, s_len, n_heads, k_pe.shape[-1]))], axis=-1)
    d_qk = k.shape[-1]
    d_pad = -(-d_qk // LANES) * LANES

    def head_pad(z):
        z = jnp.pad(z.astype(BF16), ((0, 0), (0, 0), (0, 0), (0, d_pad - d_qk)))
        return z.reshape(bsz, s_len, n_heads * d_pad)

    k = head_pad(_rms_norm(k, kv["norm_k"]))
    v = mm(c, kv["w_uv_2d"], BF16)
    scale = d_qk ** -0.5

    def attend(q):
        return flash_prompt(head_pad(q), k, v, n_heads, scale)

    return attend


def _sample_attend_fn(c_new, kpe_new, cache_c, cache_kpe, page_table, kv, n_heads, d_nope):
    bsz, t, d_c = c_new.shape
    d_r = kpe_new.shape[-1]
    d_qk = d_nope + d_r
    page = cache_c.shape[1]
    pt_flat = page_table.reshape(-1)
    npg = _pick(page_table.shape[1], (PAGES_PER_STEP, 4, 2, 1))

    k_nope_new = mm(c_new, kv["w_uk_2d"]).reshape(bsz, t, n_heads, d_nope)
    ss = jnp.sum(k_nope_new * k_nope_new, axis=-1) + jnp.sum(kpe_new * kpe_new, axis=-1)[..., None]
    s_new = lax.rsqrt(ss / d_qk + RMS_EPS)
    kn = page
    ckv_new = jnp.pad(jnp.concatenate([c_new, kpe_new], axis=-1).astype(BF16), ((0, 0), (0, kn - t), (0, 0)))
    s_new_p = jnp.pad(s_new.transpose(0, 2, 1), ((0, 0), (0, 0), (0, kn - t)), constant_values=1.0)

    s_past, ckv = paged_key_scale(cache_c, cache_kpe, pt_flat, kv["w_uk_t"], n_heads, d_qk, npg)
    n_grp = page_table.shape[1] // npg
    s_past = s_past.reshape(bsz, n_grp, n_heads, npg * page)
    ckv = ckv.reshape(bsz, n_grp, npg * page, d_c + d_r)
    g_kn, g_kr = kv["norm_k"][:d_nope], kv["norm_k"][d_nope:]
    scale = d_qk ** -0.5

    def attend(q):
        qn = (q[..., :d_nope] * g_kn).transpose(2, 0, 1, 3).reshape(n_heads, bsz * t, d_nope)
        q_lat = bmm(qn, kv["w_uk_hdc"], BF16)
        q_lat = q_lat.reshape(n_heads, bsz, t, d_c).transpose(1, 2, 0, 3)
        q_r = (q[..., d_nope:] * g_kr).astype(BF16)
        q_cat = jnp.concatenate([q_lat, q_r], axis=-1).reshape(bsz, t * n_heads, d_c + d_r)
        o_lat = paged_attention(q_cat, ckv_new, s_new_p, ckv, s_past, n_heads, t, d_c, scale)
        o_lat = o_lat.reshape(bsz, t, n_heads, d_c).transpose(2, 0, 1, 3).reshape(n_heads, bsz * t, d_c)
        o = bmm(o_lat, kv["w_uv_hcd"])
        return o.reshape(n_heads, bsz, t, -1).transpose(1, 2, 0, 3).reshape(bsz, t, -1)

    return attend


def kernel(x_prompt, x_sample, cache_kv_latent, cache_k_pe, state_wkv, state_shift, state_conv, page_table,
           ffn_norm, ffn_w_gate, ffn_w_up, ffn_conv_w, ffn_conv_b, ffn_w_down,
           rw_norm, rw_mix, rw_w0, rw_w1, rw_w2, rw_a0, rw_a1, rw_a2, rw_v0, rw_v1, rw_v2,
           rw_g1, rw_g2, rw_k_k, rw_k_a, rw_r_k, rw_wr, rw_wk, rw_wv, rw_ln_w, rw_ln_b, rw_wo,
           kv_norm, kv_w_dkv, kv_norm_c, kv_w_kr, kv_w_uk, kv_w_uv, kv_norm_k,
           mla_norm, mla_w_dq, mla_norm_cq, mla_w_uq, mla_norm_q, mla_wo):
    depth = ffn_norm.shape[0]
    n_a = rw_norm.shape[0]
    d_c, n_heads, d_nope = kv_w_uk.shape
    d_v = kv_w_uv.shape[2]
    bf = lambda z: z.astype(BF16)

    ffn = [dict(norm=ffn_norm[l], w_gate=bf(ffn_w_gate[l]), w_up=bf(ffn_w_up[l]), conv_w=ffn_conv_w[l],
                conv_b=ffn_conv_b[l], w_down=bf(ffn_w_down[l])) for l in range(depth)]
    rw = [dict(norm=rw_norm[l], mix=rw_mix[l], w0=rw_w0[l], w1=bf(rw_w1[l]), w2=bf(rw_w2[l]),
               a0=rw_a0[l], a1=bf(rw_a1[l]), a2=bf(rw_a2[l]), g1=bf(rw_g1[l]), g2=bf(rw_g2[l]),
               k_k=rw_k_k[l], k_a=rw_k_a[l], r_k=rw_r_k[l], wr=bf(rw_wr[l]), wk=bf(rw_wk[l]),
               wv=bf(rw_wv[l]), ln_w=rw_ln_w[l], ln_b=rw_ln_b[l], wo=bf(rw_wo[l]),
               v_lora=None if l == 0 else (rw_v0[l - 1], bf(rw_v1[l - 1]), bf(rw_v2[l - 1])))
          for l in range(n_a)]
    kv = dict(norm=kv_norm, norm_c=kv_norm_c, norm_k=kv_norm_k,
              w_dkv_kr=bf(jnp.concatenate([kv_w_dkv, kv_w_kr], axis=1)),
              w_uk_2d=bf(kv_w_uk.reshape(d_c, n_heads * d_nope)),
              w_uv_2d=bf(kv_w_uv.reshape(d_c, n_heads * d_v)),
              w_uk_t=bf(kv_w_uk.reshape(d_c, n_heads * d_nope).T),
              w_uk_hdc=bf(kv_w_uk.transpose(1, 2, 0)),
              w_uv_hcd=bf(kv_w_uv.transpose(1, 0, 2)))
    mla = [dict(norm=mla_norm[b], w_dq=bf(mla_w_dq[b]), norm_cq=mla_norm_cq[b],
                w_uq=bf(mla_w_uq[b].reshape(mla_w_uq.shape[1], -1)), norm_q=mla_norm_q[b],
                wo=bf(mla_wo[b])) for b in range(depth - n_a)]

    def trunk(x, pos, shift_in, wkv_in, conv_in, make_attend, lanes_are_batch):
        v_first = attend = c = k_pe = None
        shifts, wkvs, convs = [], [], []
        for layer in range(depth):
            if layer < n_a:
                x, sh, s, v_first = _rwkv7_time_mix(x, shift_in[layer], wkv_in[layer], v_first, rw[layer],
                                                    lanes_are_batch)
                shifts.append(sh)
                wkvs.append(s)
            else:
                if layer == n_a:
                    c, k_pe = _mla_kv_side(x, pos, kv)
                    attend = make_attend(c, k_pe)
                p = mla[layer - n_a]
                q = _mla_query(x, pos, p, n_heads, d_nope)
                x = mm(attend(q), p["wo"], res=x)
            x, cv = _conv_ffn(x, conv_in[layer], ffn[layer])
            convs.append(cv)
        return x, c, k_pe, jnp.stack(wkvs), jnp.stack(shifts), jnp.stack(convs)

    bp, sp, d = x_prompt.shape
    n_rw_heads, n = rw_r_k.shape[1:]
    d_ff = ffn_w_gate.shape[2]
    out_p = trunk(
        x_prompt, jnp.arange(sp),
        jnp.zeros((n_a, bp, d), F32), jnp.zeros((n_a, bp, n_rw_heads, n, n), F32),
        jnp.zeros((depth, bp, CONV_W - 1, d_ff), F32),
        lambda c, kp: _prompt_attend_fn(c, kp, kv, n_heads, d_nope), False)

    past_len = page_table.shape[1] * cache_kv_latent.shape[1]
    out_s = trunk(
        x_sample, past_len + jnp.arange(x_sample.shape[1]), state_shift, state_wkv, state_conv,
        lambda c, kp: _sample_attend_fn(c, kp, cache_kv_latent, cache_k_pe, page_table, kv, n_heads, d_nope),
        True)

    return (out_p[0], out_s[0]) + tuple(out_p[1:]) + tuple(out_s[1:])
```

```python
import functools

import jax
import jax.numpy as jnp
from jax import lax
from jax.experimental import pallas as pl
from jax.experimental.pallas import tpu as pltpu

F32 = jnp.float32
BF16 = jnp.bfloat16

RMS_EPS = 1e-6
ROPE_BASE = 10000.0
GN_EPS_PER_CHANNEL = 1e-5
CONV_W = 3

VMEM_LIMIT_BYTES = 52 * 1024 * 1024
LANES = 128
MXU_WIDTH = 256
PAGES_PER_STEP = 8
ROW_BLOCK = 1024
WKV_CHUNK = 64
_ROW_BLOCKS = (1024, 512, 256, 128, 64, 32, 16, 8)


def _pick(n, candidates):
    for c in candidates:
        if c <= n and n % c == 0:
            return c
    return n


def _cparams(sem):
    return pltpu.CompilerParams(dimension_semantics=sem, vmem_limit_bytes=VMEM_LIMIT_BYTES)


def _mm_kernel(x_ref, w_ref, o_ref):
    o_ref[0] = jnp.dot(x_ref[0].astype(BF16), w_ref[0],
                       preferred_element_type=F32).astype(o_ref.dtype)


def _mm_res_kernel(x_ref, w_ref, r_ref, o_ref):
    o_ref[0] = r_ref[0] + jnp.dot(x_ref[0].astype(BF16), w_ref[0], preferred_element_type=F32)


def bmm(x, w, out_dtype=F32, res=None):
    g, m, k = x.shape
    n = w.shape[2]
    bm = _pick(m, (1024, 512, 256, 128, 64, 32, 16, 8))
    bn = n if n <= 1024 else _pick(n, (512, 256, 128))
    out_spec = pl.BlockSpec((1, bm, bn), lambda a, i, j: (a, i, j))
    in_specs = [pl.BlockSpec((1, bm, k), lambda a, i, j: (a, i, 0)),
                pl.BlockSpec((1, k, bn), lambda a, i, j: (a, 0, j))]
    return pl.pallas_call(
        _mm_kernel if res is None else _mm_res_kernel,
        grid=(g, m // bm, n // bn),
        in_specs=in_specs if res is None else in_specs + [out_spec],
        out_specs=out_spec,
        out_shape=jax.ShapeDtypeStruct((g, m, n), out_dtype),
        compiler_params=_cparams(("parallel", "parallel", "arbitrary")),
        name="bmm",
    )(*((x, w) if res is None else (x, w, res)))


def mm(x, w, out_dtype=F32, res=None):
    lead = x.shape[:-1]
    n = w.shape[1]
    out = bmm(x.reshape((1, -1, x.shape[-1])), w[None], out_dtype,
              None if res is None else res.reshape((1, -1, n)))
    return out.reshape(lead + (n,))


def _ffn_in_kernel(*refs, t_len, bm, seq_in_block):
    if seq_in_block:
        x_ref, wg_ref, wu_ref, cw_ref, cb_ref, e_ref, hid_ref, tail_ref = refs
    else:
        x_ref, xp_ref, wg_ref, wu_ref, cw_ref, cb_ref, e_ref, hid_ref, tail_ref = refs
    x = x_ref[...]
    wg = wg_ref[...]
    u = jnp.dot(x, wg, preferred_element_type=F32)
    up = jnp.dot(x, wu_ref[...], preferred_element_type=F32)
    row = lax.broadcasted_iota(jnp.int32, u.shape, 0)
    if seq_in_block:
        t = row % t_len
        u1 = jnp.where(t < 1, e_ref[0], pltpu.roll(u, 1, axis=0))
        u2 = jnp.where(t < 2, e_ref[1], pltpu.roll(u, 2, axis=0))
        tail_ref[...] = u
    else:
        prev8 = jnp.dot(xp_ref[...], wg, preferred_element_type=F32)
        seq_start = (pl.program_id(0) * bm) % t_len == 0
        prev8 = jnp.where(seq_start, e_ref[0], prev8)
        u1 = jnp.where(row < 1, prev8[7:8], pltpu.roll(u, 1, axis=0))
        u2 = jnp.where(row < 1, prev8[6:7], jnp.where(row < 2, prev8[7:8], pltpu.roll(u, 2, axis=0)))
        tail_ref[0] = u[bm - 8:bm]
    conv = cb_ref[...] + cw_ref[0:1] * u2 + cw_ref[1:2] * u1 + cw_ref[2:3] * u
    hid_ref[...] = (conv * jax.nn.sigmoid(conv) * up).astype(hid_ref.dtype)


def ffn_in(h, prev, w_gate, w_up, conv_w, conv_b):
    bsz, t_len, d = h.shape
    f = w_gate.shape[1]
    m = bsz * t_len
    x = h.reshape(m, d)
    bn = _pick(f, (512, 256, 128))
    seq_in_block = t_len < ROW_BLOCK
    if seq_in_block:
        bm = _pick(m, tuple(c for c in _ROW_BLOCKS if c <= ROW_BLOCK and c % t_len == 0))
        tpos = jnp.arange(t_len)[None, :, None]
        e = jnp.stack([jnp.broadcast_to(prev[:, 1:2], (bsz, t_len, f)),
                       jnp.where(tpos == 0, prev[:, 0:1], prev[:, 1:2])]).reshape(2, m, f)
        lead = []
        lead_specs = []
        e_spec = pl.BlockSpec((2, bm, bn), lambda i, j: (0, i, j))
        tail_shape = jax.ShapeDtypeStruct((m, f), F32)
        tail_spec = pl.BlockSpec((bm, bn), lambda i, j: (i, j))
    else:
        bm = _pick(t_len, tuple(c for c in _ROW_BLOCKS if c <= ROW_BLOCK))
        e = jnp.pad(prev, ((0, 0), (6, 0), (0, 0)))
        lead = [x]
        lead_specs = [pl.BlockSpec((8, d), lambda i, j: (jnp.maximum(i * (bm // 8) - 1, 0), 0))]
        e_spec = pl.BlockSpec((1, 8, bn), lambda i, j: ((i * bm) // t_len, 0, j))
        tail_shape = jax.ShapeDtypeStruct((m // bm, 8, f), F32)
        tail_spec = pl.BlockSpec((1, 8, bn), lambda i, j: (i, 0, j))
    w_spec = pl.BlockSpec((d, bn), lambda i, j: (0, j))
    hid, tail = pl.pallas_call(
        functools.partial(_ffn_in_kernel, t_len=t_len, bm=bm, seq_in_block=seq_in_block),
        grid=(m // bm, f // bn),
        in_specs=([pl.BlockSpec((bm, d), lambda i, j: (i, 0))] + lead_specs
                  + [w_spec, w_spec, pl.BlockSpec((CONV_W, bn), lambda i, j: (0, j)),
                     pl.BlockSpec((1, bn), lambda i, j: (0, j)), e_spec]),
        out_specs=[pl.BlockSpec((bm, bn), lambda i, j: (i, j)), tail_spec],
        out_shape=[jax.ShapeDtypeStruct((m, f), BF16), tail_shape],
        compiler_params=_cparams(("parallel", "arbitrary")),
        name="ffn_in",
    )(x, *lead, w_gate, w_up, conv_w, conv_b.reshape(1, f), e)
    if seq_in_block:
        state = tail.reshape(bsz, t_len, f)[:, t_len - 2:]
    else:
        state = tail.reshape(bsz, t_len // bm, 8, f)[:, -1, 6:]
    return hid, state


def _dot(a, b, dims=((1,), (0,))):
    return lax.dot_general(a.astype(BF16), b.astype(BF16), (dims, ((), ())), preferred_element_type=F32)


_NT = ((1,), (1,))
_TN = ((0,), (0,))


def _wkv_chunk_kernel(r_ref, ld_ref, k_ref, v_ref, kk_ref, as_ref, s0_ref, y_ref, sout_ref, s_sc, *,
                      c, hb, n, ng, nb):
    w = hb * n
    rb = hb * c
    rows = nb * rb
    first = pl.program_id(2) == 0
    last = pl.program_id(2) == pl.num_programs(2) - 1
    ri = lax.broadcasted_iota(jnp.int32, (rb, w), 0)
    ci = lax.broadcasted_iota(jnp.int32, (rb, w), 1)
    head_mask = (ri // c) == (ci // n)
    bd = lambda z: jnp.where(head_mask, jnp.concatenate([z] * hb, axis=0), 0.0)
    trow = lax.broadcasted_iota(jnp.int32, (c, w), 0)
    tr = lax.broadcasted_iota(jnp.int32, (rows, rows), 0)
    tc_ = lax.broadcasted_iota(jnp.int32, (rows, rows), 1)
    same = (tr // c) == (tc_ // c)
    strict = same & ((tr % c) > (tc_ % c))
    incl = same & ((tr % c) >= (tc_ % c))
    eye = jnp.where(tr == tc_, 1.0, 0.0)
    grp = range(ng)
    bat = range(nb)
    lanes = [slice(gi * w, (gi + 1) * w) for gi in grp]
    cat = lambda parts: parts[0] if len(parts) == 1 else jnp.concatenate(parts, axis=0)
    rsl = lambda z, bi: z[bi * rb:(bi + 1) * rb]

    @pl.when(first)
    def _():
        sr = lax.broadcasted_iota(jnp.int32, (w, w), 0)
        sc_ = lax.broadcasted_iota(jnp.int32, (w, w), 1)
        for gi in grp:
            for bi in bat:
                s0 = jnp.concatenate([s0_ref[bi, gi * hb + h] for h in range(hb)], axis=0)
                s_sc[gi * nb + bi] = jnp.where((sr // n) == (sc_ // n),
                                               jnp.concatenate([s0] * hb, axis=1), 0.0)

    def prep(gi):
        parts = []
        for bi in bat:
            r, ld, k, v, kk, asig = (ref[bi, :, lanes[gi]]
                                     for ref in (r_ref, ld_ref, k_ref, v_ref, kk_ref, as_ref))
            lam = ld
            shift = 1
            while shift < c:
                lam = lam + jnp.where(trow >= shift, pltpu.roll(lam, shift, axis=0), 0.0)
                shift *= 2
            g_inv = jnp.exp(-lam)
            parts.append((bd(-kk * jnp.exp(lam - ld)), bd(r * jnp.exp(lam)), bd(kk * asig * g_inv),
                          bd(k * g_inv), bd(v), jnp.exp(lam[c - 1:c])))
        return tuple(cat([pt[i] for pt in parts]) for i in range(5)) + ([pt[5] for pt in parts],)

    a_bd, r_bd, b_bd, k_bd, v_bd, g_end = zip(*[prep(gi) for gi in grp])
    p = [_dot(jnp.concatenate([a_bd[gi], r_bd[gi]], axis=0),
              jnp.concatenate([b_bd[gi], k_bd[gi]], axis=0), _NT) for gi in grp]
    l_pow = [jnp.where(strict, p[gi][:rows, :rows], 0.0) for gi in grp]
    t_inv = [eye + l_pow[gi] for gi in grp]
    for _ in range(c.bit_length() - 2):
        l_pow = [_dot(l_pow[gi], l_pow[gi]) for gi in grp]
        t_inv = [t_inv[gi] + _dot(t_inv[gi], l_pow[gi]) for gi in grp]
    s = [[s_sc[gi * nb + bi] for bi in bat] for gi in grp]
    xs = [[_dot(jnp.concatenate([rsl(a_bd[gi], bi), rsl(r_bd[gi], bi)], axis=0), s[gi][bi], _NT)
           for bi in bat] for gi in grp]
    xs_a = [cat([xs[gi][bi][:rb] for bi in bat]) for gi in grp]
    xs_r = [cat([xs[gi][bi][rb:] for bi in bat]) for gi in grp]
    lkv = [_dot(jnp.where(strict, p[gi][:rows, rows:], 0.0), v_bd[gi]) for gi in grp]
    pkv = [_dot(jnp.where(incl, p[gi][rows:, rows:], 0.0), v_bd[gi]) for gi in grp]
    u = [_dot(t_inv[gi], xs_a[gi] + lkv[gi]) for gi in grp]
    y_bd = [xs_r[gi] + pkv[gi] + _dot(jnp.where(incl, p[gi][rows:, :rows], 0.0), u[gi]) for gi in grp]
    s_new = [[s[gi][bi] * g_end[gi][bi]
              + _dot(jnp.concatenate([rsl(u[gi], bi), rsl(v_bd[gi], bi)], axis=0),
                     jnp.concatenate([rsl(b_bd[gi], bi), rsl(k_bd[gi], bi)], axis=0) * g_end[gi][bi], _TN)
              for bi in bat] for gi in grp]
    for gi in grp:
        for bi in bat:
            y = y_bd[gi][bi * rb:bi * rb + c]
            for h in range(1, hb):
                y = y + y_bd[gi][bi * rb + h * c:bi * rb + (h + 1) * c]
            y_ref[bi, :, lanes[gi]] = y
            s_sc[gi * nb + bi] = s_new[gi][bi]

    @pl.when(last)
    def _():
        for gi in grp:
            for bi in bat:
                for h in range(hb):
                    sout_ref[bi, gi * hb + h] = s_new[gi][bi][h * n:(h + 1) * n, h * n:(h + 1) * n]


def wkv_chunked(r, ld, k, v, kk, asig, s0, n_heads):
    bsz, t, d = r.shape
    n = d // n_heads
    hb = _pick(n_heads, (4, 2, 1))
    ng = _pick(n_heads // hb, (4, 2, 1))
    c = _pick(t, (WKV_CHUNK, 32, 16, 8))
    nb = _pick(bsz, tuple(x for x in (8, 4, 2, 1) if x * hb * c <= MXU_WIDTH))
    seq_spec = pl.BlockSpec((nb, c, ng * hb * n), lambda b, q, ch: (b, ch, q))
    st_spec = pl.BlockSpec((nb, ng * hb, n, n), lambda b, q, ch: (b, q, 0, 0))
    return pl.pallas_call(
        functools.partial(_wkv_chunk_kernel, c=c, hb=hb, n=n, ng=ng, nb=nb),
        grid=(bsz // nb, n_heads // (hb * ng), t // c),
        in_specs=[seq_spec] * 6 + [st_spec],
        out_specs=[seq_spec, st_spec],
        out_shape=[jax.ShapeDtypeStruct((bsz, t, d), F32),
                   jax.ShapeDtypeStruct((bsz, n_heads, n, n), F32)],
        scratch_shapes=[pltpu.VMEM((ng * nb, hb * n, hb * n), F32)],
        compiler_params=_cparams(("parallel", "parallel", "arbitrary")),
        name="wkv_chunked",
    )(r, ld, k, v, kk, asig, s0)


def _flash_kernel(q_ref, k_ref, v_ref, o_ref, *, bq, n_q, scale):
    qi = pl.program_id(2)
    q = q_ref[0]
    dn = (((1,), (1,)), ((), ()))

    for i in range(n_q):
        @pl.when(qi == i)
        def _(i=i):
            kv_len = (i + 1) * bq
            s = lax.dot_general(q, k_ref[0, :kv_len, :], dn, preferred_element_type=F32) * scale
            qpos = i * bq + lax.broadcasted_iota(jnp.int32, (bq, kv_len), 0)
            kpos = lax.broadcasted_iota(jnp.int32, (bq, kv_len), 1)
            s = jnp.where(kpos <= qpos, s, -jnp.inf)
            p = jnp.exp(s - jnp.max(s, axis=-1, keepdims=True))
            l = jnp.sum(p, axis=-1, keepdims=True)
            o = jnp.dot(p.astype(BF16), v_ref[0, :kv_len, :], preferred_element_type=F32)
            o_ref[0] = (o / l).astype(o_ref.dtype)


def flash_prompt(q, k, v, n_heads, scale):
    bsz, s_len, _ = q.shape
    dp = q.shape[-1] // n_heads
    dv = v.shape[-1] // n_heads
    bq = _pick(s_len, (512, 256, 128, 64, 32, 16, 8))
    n_q = s_len // bq
    return pl.pallas_call(
        functools.partial(_flash_kernel, bq=bq, n_q=n_q, scale=scale),
        grid=(bsz, n_heads, n_q),
        in_specs=[pl.BlockSpec((1, bq, dp), lambda b, h, i: (b, i, h)),
                  pl.BlockSpec((1, s_len, dp), lambda b, h, i: (b, 0, h)),
                  pl.BlockSpec((1, s_len, dv), lambda b, h, i: (b, 0, h))],
        out_specs=pl.BlockSpec((1, bq, dv), lambda b, h, i: (b, i, h)),
        out_shape=jax.ShapeDtypeStruct((bsz, s_len, n_heads * dv), BF16),
        compiler_params=_cparams(("parallel", "parallel", "arbitrary")),
        name="flash_prompt",
    )(q, k, v)


def _sum_rows_to_lanes(x):
    hi = x.astype(BF16)
    lo = (x - hi.astype(F32)).astype(BF16)
    ones = jnp.ones((8, x.shape[1]), BF16)
    dn = (((1,), (1,)), ((), ()))
    out = (lax.dot_general(ones, hi, dn, preferred_element_type=F32)
           + lax.dot_general(ones, lo, dn, preferred_element_type=F32))
    return out[0:1]


def _key_scale_block(wt, c, kpe, n_heads, d_qk):
    k_t = lax.dot_general(wt, c.astype(BF16), (((1,), (1,)), ((), ())), preferred_element_type=F32)
    sq = k_t * k_t
    ss = jnp.sum(sq.reshape(n_heads, sq.shape[0] // n_heads, sq.shape[1]), axis=1)
    kpe = kpe.astype(F32)
    pe = _sum_rows_to_lanes(kpe * kpe)
    return lax.rsqrt((ss + pe) / d_qk + RMS_EPS)


def _kscale_kernel(pt_ref, *refs, npg, n_heads, d_qk):
    del pt_ref
    c_refs = refs[:npg]
    kpe_refs = refs[npg:2 * npg]
    wt_ref = refs[2 * npg]
    s_ref, ckv_ref = refs[2 * npg + 1:]
    page, d_c = c_refs[0].shape[1:]
    d_r = kpe_refs[0].shape[2]
    pair = 2 if npg % 2 == 0 else 1
    for p in range(0, npg, pair):
        rows = slice(p * page, (p + pair) * page)
        c = jnp.concatenate([c_refs[p + q][0] for q in range(pair)], axis=0)
        kpe = jnp.concatenate([kpe_refs[p + q][0] for q in range(pair)], axis=0)
        s_ref[0, :, rows] = _key_scale_block(wt_ref[...], c, kpe, n_heads, d_qk)
        ckv_ref[0, rows, 0:d_c] = c.astype(BF16)
        ckv_ref[0, rows, d_c:d_c + d_r] = kpe.astype(BF16)


def paged_key_scale(cache_c, cache_kpe, page_table_flat, w_uk_t, n_heads, d_qk, npg):
    n_used = page_table_flat.shape[0]
    page, d_c = cache_c.shape[1:]
    d_r = cache_kpe.shape[2]
    n_grp = n_used // npg

    def page_spec(width, p):
        return pl.BlockSpec((1, page, width), lambda g, pt, p=p: (pt[g * npg + p], 0, 0))

    in_specs = ([page_spec(d_c, p) for p in range(npg)] + [page_spec(d_r, p) for p in range(npg)]
                + [pl.BlockSpec(w_uk_t.shape, lambda g, pt: (0, 0))])
    grid_spec = pltpu.PrefetchScalarGridSpec(
        num_scalar_prefetch=1, grid=(n_grp,), in_specs=in_specs,
        out_specs=[pl.BlockSpec((1, n_heads, npg * page), lambda g, pt: (g, 0, 0)),
                   pl.BlockSpec((1, npg * page, d_c + d_r), lambda g, pt: (g, 0, 0))])
    return pl.pallas_call(
        functools.partial(_kscale_kernel, npg=npg, n_heads=n_heads, d_qk=d_qk),
        grid_spec=grid_spec,
        out_shape=[jax.ShapeDtypeStruct((n_grp, n_heads, npg * page), F32),
                   jax.ShapeDtypeStruct((n_grp, npg * page, d_c + d_r), BF16)],
        compiler_params=_cparams(("arbitrary",)),
        name="paged_key_scale",
    )(page_table_flat, *([cache_c] * npg), *([cache_kpe] * npg), w_uk_t)


def _paged_attn_kernel(q_ref, new_ref, snew_ref, ckv_ref, s_ref, o_ref, m_sc, l_sc, acc_sc, *,
                       bb, n_heads, t_len, d_c, scale):
    g = pl.program_id(1)
    rows = n_heads * t_len
    dn = (((1,), (1,)), ((), ()))

    def process(ckv, key_scale, mask):
        bs = range(bb)
        sc = [lax.dot_general(q_ref[b], ckv[b], dn, preferred_element_type=F32) for b in bs]
        sc = [sc[b] * (jnp.concatenate([key_scale[b]] * t_len, axis=0) * scale) for b in bs]
        if mask is not None:
            sc = [jnp.where(mask, sc[b], -jnp.inf) for b in bs]
        m_new = [jnp.maximum(m_sc[b], jnp.max(sc[b], axis=-1, keepdims=True)) for b in bs]
        alpha = [jnp.exp(m_sc[b] - m_new[b]) for b in bs]
        p = [jnp.exp(sc[b] - m_new[b]) for b in bs]
        pv = [jnp.dot(p[b].astype(BF16), ckv[b][:, :d_c], preferred_element_type=F32) for b in bs]
        for b in bs:
            l_sc[b] = l_sc[b] * alpha[b] + jnp.sum(p[b], axis=-1, keepdims=True)
            acc_sc[b] = acc_sc[b] * alpha[b] + pv[b]
            m_sc[b] = m_new[b]

    @pl.when(g == 0)
    def _():
        m_sc[...] = jnp.full(m_sc.shape, -jnp.inf, F32)
        l_sc[...] = jnp.zeros(l_sc.shape, F32)
        acc_sc[...] = jnp.zeros(acc_sc.shape, F32)
        nk = new_ref.shape[1]
        t_row = lax.broadcasted_iota(jnp.int32, (rows, nk), 0) // n_heads
        key = lax.broadcasted_iota(jnp.int32, (rows, nk), 1)
        process([new_ref[b] for b in range(bb)], [snew_ref[b] for b in range(bb)], key <= t_row)

    process([ckv_ref[b, 0] for b in range(bb)], [s_ref[b, 0] for b in range(bb)], None)

    @pl.when(g == pl.num_programs(1) - 1)
    def _():
        for b in range(bb):
            o_ref[b] = acc_sc[b] / l_sc[b]


def paged_attention(q_cat, ckv_new, s_new, ckv, s_past, n_heads, t_len, d_c, scale):
    bsz, rows, d_cat = q_cat.shape
    _, n_grp, nk, _ = ckv.shape
    kn = ckv_new.shape[1]
    bb = _pick(bsz, (4, 2, 1))

    def per_b(shape):
        return pl.BlockSpec((bb,) + shape, lambda b, g: (b, 0, 0))

    return pl.pallas_call(
        functools.partial(_paged_attn_kernel, bb=bb, n_heads=n_heads, t_len=t_len, d_c=d_c, scale=scale),
        grid=(bsz // bb, n_grp),
        in_specs=[per_b((rows, d_cat)), per_b((kn, d_cat)), per_b((n_heads, kn)),
                  pl.BlockSpec((bb, 1, nk, d_cat), lambda b, g: (b, g, 0, 0)),
                  pl.BlockSpec((bb, 1, n_heads, nk), lambda b, g: (b, g, 0, 0))],
        out_specs=per_b((rows, d_c)),
        out_shape=jax.ShapeDtypeStruct((bsz, rows, d_c), F32),
        scratch_shapes=[pltpu.VMEM((bb, rows, 1), F32), pltpu.VMEM((bb, rows, 1), F32),
                        pltpu.VMEM((bb, rows, d_c), F32)],
        compiler_params=_cparams(("parallel", "arbitrary")),
        name="paged_attention",
    )(q_cat, ckv_new, s_new, ckv, s_past)


def _rms_norm(x, g):
    return x * lax.rsqrt(jnp.mean(x * x, axis=-1, keepdims=True) + RMS_EPS) * g


def _rope(x, pos):
    half = x.shape[-1] // 2
    inv = ROPE_BASE ** (-jnp.arange(half, dtype=F32) / half)
    ang = pos.astype(F32)[:, None] * inv[None, :]
    cos = jnp.cos(ang)[None, :, None, :]
    sin = jnp.sin(ang)[None, :, None, :]
    x1, x2 = x[..., :half], x[..., half:]
    return jnp.concatenate([x1 * cos - x2 * sin, x1 * sin + x2 * cos], axis=-1)


def _conv_ffn(x, prev, p):
    h = _rms_norm(x, p["norm"]).astype(BF16)
    hid, state = ffn_in(h, prev, p["w_gate"], p["w_up"], p["conv_w"], p["conv_b"])
    return mm(hid, p["w_down"], res=x.reshape(hid.shape[0], -1)).reshape(x.shape), state


def _rwkv7_time_mix(x, shift_prev, wkv_prev, v_first, p):
    bsz, t, d = x.shape
    n_heads, n = p["r_k"].shape
    h = _rms_norm(x, p["norm"])
    prev = jnp.concatenate([shift_prev[:, None, :], h[:, :-1]], axis=1)
    xx = prev - h
    xr, xw, xk, xv, xa, xg = ((h + xx * p["mix"][i]).astype(BF16) for i in range(6))
    r = mm(xr, p["wr"])
    k = mm(xk, p["wk"])
    v = mm(xv, p["wv"])
    w_log = -jax.nn.softplus(-(p["w0"] + mm(jnp.tanh(mm(xw, p["w1"])), p["w2"]))) - 0.5
    log_decay = -jnp.exp(w_log)
    a =jax.nn.sigmoid(p["a0"] + mm(mm(xa, p["a1"]), p["a2"]))
    g = mm(jax.nn.sigmoid(mm(xg, p["g1"])), p["g2"])
    if p["v_lora"] is None:
        v_first = v
    else:
        v0, v1, v2 = p["v_lora"]
        v = v + (v_first - v) * jax.nn.sigmoid(v0 + mm(mm(xv, v1), v2))

    def hd(z):
        return z.reshape(bsz, t, n_heads, n)

    kk = hd(k * p["k_k"])
    kk = kk / jnp.maximum(jnp.sqrt(jnp.sum(kk * kk, axis=-1, keepdims=True)), 1e-12)
    k = k * (1 + (a - 1) * p["k_a"])
    rh, kh, vh = hd(r), hd(k), hd(v)
    y, s = wkv_chunked(r, log_decay, k, v, kk.reshape(bsz, t, d), a, wkv_prev, n_heads)
    y = hd(y)
    mu =jnp.mean(y, axis=-1, keepdims=True)
    var = jnp.mean((y - mu) ** 2, axis=-1, keepdims=True)
    y = ((y - mu) * lax.rsqrt(var + n * GN_EPS_PER_CHANNEL)).reshape(bsz, t, d) * p["ln_w"] + p["ln_b"]
    y = y + (jnp.sum(rh * kh * p["r_k"], axis=-1, keepdims=True) * vh).reshape(bsz, t, d)
    return mm((y * g).astype(BF16), p["wo"], res=x), h[:, -1], s, v_first


def _mla_kv_side(x, pos, kv):
    u = _rms_norm(x, kv["norm"]).astype(BF16)
    d_c = kv["norm_c"].shape[0]
    ckr = mm(u, kv["w_dkv_kr"])
    c = _rms_norm(ckr[..., :d_c], kv["norm_c"])
    k_pe = _rope(ckr[..., d_c:][:, :, None, :], pos)[:, :, 0, :]
    return c, k_pe


def _mla_query(x, pos, p, n_heads, d_nope):
    bsz, t, _ = x.shape
    h = _rms_norm(x, p["norm"]).astype(BF16)
    cq = _rms_norm(mm(h, p["w_dq"]), p["norm_cq"]).astype(BF16)
    q = mm(cq, p["w_uq"]).reshape(bsz, t, n_heads, -1)
    q = jnp.concatenate([q[..., :d_nope], _rope(q[..., d_nope:], pos)], axis=-1)
    return _rms_norm(q, p["norm_q"])


def _prompt_attend_fn(c, k_pe, kv, n_heads, d_nope):
    bsz, s_len, _ = c.shape
    k_nope = mm(c, kv["w_uk_2d"]).reshape(bsz, s_len, n_heads, d_nope)
    k = jnp.concatenate(
        [k_nope, jnp.broadcast_to(k_pe[:, :, None, :], (bsz, s_len, n_heads, k_pe.shape[-1]))], axis=-1)
    d_qk = k.shape[-1]
    d_pad = -(-d_qk // LANES) * LANES

    def head_pad(z):
        z = jnp.pad(z.astype(BF16), ((0, 0), (0, 0), (0, 0), (0, d_pad - d_qk)))
        return z.reshape(bsz, s_len, n_heads * d_pad)

    k = head_pad(_rms_norm(k, kv["norm_k"]))
    v = mm(c, kv["w_uv_2d"], BF16)
    scale = d_qk ** -0.5

    def attend(q):
        return flash_prompt(head_pad(q), k, v, n_heads, scale)

    return attend


def _sample_attend_fn(c_new, kpe_new, cache_c, cache_kpe, page_table, kv, n_heads, d_nope):
    bsz, t, d_c = c_new.shape
    d_r = kpe_new.shape[-1]
    d_qk = d_nope + d_r
    page = cache_c.shape[1]
    pt_flat = page_table.reshape(-1)
    npg = _pick(page_table.shape[1], (PAGES_PER_STEP, 4, 2, 1))

    k_nope_new = mm(c_new, kv["w_uk_2d"]).reshape(bsz, t, n_heads, d_nope)
    ss = jnp.sum(k_nope_new * k_nope_new, axis=-1) + jnp.sum(kpe_new * kpe_new, axis=-1)[..., None]
    s_new = lax.rsqrt(ss / d_qk + RMS_EPS)
    kn = page
    ckv_new = jnp.pad(jnp.concatenate([c_new, kpe_new], axis=-1).astype(BF16), ((0, 0), (0, kn - t), (0, 0)))
    s_new_p = jnp.pad(s_new.transpose(0, 2, 1), ((0, 0), (0, 0), (0, kn - t)), constant_values=1.0)

    s_past, ckv = paged_key_scale(cache_c, cache_kpe, pt_flat, kv["w_uk_t"], n_heads, d_qk, npg)
    n_grp = page_table.shape[1] // npg
    s_past = s_past.reshape(bsz, n_grp, n_heads, npg * page)
    ckv = ckv.reshape(bsz, n_grp, npg * page, d_c + d_r)
    g_kn, g_kr = kv["norm_k"][:d_nope], kv["norm_k"][d_nope:]
    scale = d_qk ** -0.5

    def attend(q):
        qn = (q[..., :d_nope] * g_kn).transpose(2, 0, 1, 3).reshape(n_heads, bsz * t, d_nope)
        q_lat = bmm(qn, kv["w_uk_hdc"], BF16)
        q_lat = q_lat.reshape(n_heads, bsz, t, d_c).transpose(1, 2, 0, 3)
        q_r = (q[..., d_nope:] * g_kr).astype(BF16)
        q_cat = jnp.concatenate([q_lat, q_r], axis=-1).reshape(bsz, t * n_heads, d_c + d_r)
        o_lat = paged_attention(q_cat, ckv_new, s_new_p, ckv, s_past, n_heads, t, d_c, scale)
        o_lat = o_lat.reshape(bsz, t, n_heads, d_c).transpose(2, 0, 1, 3).reshape(n_heads, bsz * t, d_c)
        o = bmm(o_lat, kv["w_uv_hcd"])
        return o.reshape(n_heads, bsz, t, -1).transpose(1, 2, 0, 3).reshape(bsz, t, -1)

    return attend


def kernel(x_prompt, x_sample, cache_kv_latent, cache_k_pe, state_wkv, state_shift, state_conv, page_table,
           ffn_norm, ffn_w_gate, ffn_w_up, ffn_conv_w, ffn_conv_b, ffn_w_down,
           rw_norm, rw_mix, rw_w0, rw_w1, rw_w2, rw_a0, rw_a1, rw_a2, rw_v0, rw_v1, rw_v2,
           rw_g1, rw_g2, rw_k_k, rw_k_a, rw_r_k, rw_wr, rw_wk, rw_wv, rw_ln_w, rw_ln_b, rw_wo,
           kv_norm, kv_w_dkv, kv_norm_c, kv_w_kr, kv_w_uk, kv_w_uv, kv_norm_k,
           mla_norm, mla_w_dq, mla_norm_cq, mla_w_uq, mla_norm_q, mla_wo):
    depth = ffn_norm.shape[0]
    n_a = rw_norm.shape[0]
    d_c, n_heads, d_nope = kv_w_uk.shape
    d_v = kv_w_uv.shape[2]
    bf = lambda z: z.astype(BF16)

    ffn = [dict(norm=ffn_norm[l], w_gate=bf(ffn_w_gate[l]), w_up=bf(ffn_w_up[l]), conv_w=ffn_conv_w[l],
                conv_b=ffn_conv_b[l], w_down=bf(ffn_w_down[l])) for l in range(depth)]
    rw = [dict(norm=rw_norm[l], mix=rw_mix[l], w0=rw_w0[l], w1=bf(rw_w1[l]), w2=bf(rw_w2[l]),
               a0=rw_a0[l], a1=bf(rw_a1[l]), a2=bf(rw_a2[l]), g1=bf(rw_g1[l]), g2=bf(rw_g2[l]),
               k_k=rw_k_k[l], k_a=rw_k_a[l], r_k=rw_r_k[l], wr=bf(rw_wr[l]), wk=bf(rw_wk[l]),
               wv=bf(rw_wv[l]), ln_w=rw_ln_w[l], ln_b=rw_ln_b[l], wo=bf(rw_wo[l]),
               v_lora=None if l == 0 else (rw_v0[l - 1], bf(rw_v1[l - 1]), bf(rw_v2[l - 1])))
          for l in range(n_a)]
    kv = dict(norm=kv_norm, norm_c=kv_norm_c, norm_k=kv_norm_k,
              w_dkv_kr=bf(jnp.concatenate([kv_w_dkv, kv_w_kr], axis=1)),
              w_uk_2d=bf(kv_w_uk.reshape(d_c, n_heads * d_nope)),
              w_uv_2d=bf(kv_w_uv.reshape(d_c, n_heads * d_v)),
              w_uk_t=bf(kv_w_uk.reshape(d_c, n_heads * d_nope).T),
              w_uk_hdc=bf(kv_w_uk.transpose(1, 2, 0)),
              w_uv_hcd=bf(kv_w_uv.transpose(1, 0, 2)))
    mla = [dict(norm=mla_norm[b], w_dq=bf(mla_w_dq[b]), norm_cq=mla_norm_cq[b],
                w_uq=bf(mla_w_uq[b].reshape(mla_w_uq.shape[1], -1)), norm_q=mla_norm_q[b],
                wo=bf(mla_wo[b])) for b in range(depth - n_a)]

    def trunk(x, pos, shift_in, wkv_in, conv_in, make_attend):
        v_first = attend = c = k_pe = None
        shifts, wkvs, convs = [], [], []
        for layer in range(depth):
            if layer < n_a:
                x, sh, s, v_first = _rwkv7_time_mix(x, shift_in[layer], wkv_in[layer], v_first, rw[layer])
                shifts.append(sh)
                wkvs.append(s)
            else:
                if layer == n_a:
                    c, k_pe = _mla_kv_side(x, pos, kv)
                    attend = make_attend(c, k_pe)
                p = mla[layer - n_a]
                q = _mla_query(x, pos, p, n_heads, d_nope)
                x = mm(attend(q), p["wo"], res=x)
            x, cv = _conv_ffn(x, conv_in[layer], ffn[layer])
            convs.append(cv)
        return x, c, k_pe, jnp.stack(wkvs), jnp.stack(shifts), jnp.stack(convs)

    bp, sp, d = x_prompt.shape
    n_rw_heads, n = rw_r_k.shape[1:]
    d_ff = ffn_w_gate.shape[2]
    out_p = trunk(
        x_prompt, jnp.arange(sp),
        jnp.zeros((n_a, bp, d), F32), jnp.zeros((n_a, bp, n_rw_heads, n, n), F32),
        jnp.zeros((depth, bp, CONV_W - 1, d_ff), F32),
        lambda c, kp: _prompt_attend_fn(c, kp, kv, n_heads, d_nope))

    past_len = page_table.shape[1] * cache_kv_latent.shape[1]
    out_s = trunk(
        x_sample, past_len + jnp.arange(x_sample.shape[1]), state_shift, state_wkv, state_conv,
        lambda c, kp: _sample_attend_fn(c, kp, cache_kv_latent, cache_k_pe, page_table, kv, n_heads, d_nope))

    return (out_p[0], out_s[0]) + tuple(out_p[1:]) + tuple(out_s[1:])
```

```python
import functools

import jax
import jax.numpy as jnp
from jax import lax
from jax.experimental import pallas as pl
from jax.experimental.pallas import tpu as pltpu

F32 = jnp.float32
BF16 = jnp.bfloat16

RMS_EPS = 1e-6
ROPE_BASE = 10000.0
GN_EPS_PER_CHANNEL = 1e-5
CONV_W = 3

VMEM_LIMIT_BYTES = 52 * 1024 * 1024
LANES = 128
MXU_WIDTH = 256
PAGES_PER_STEP = 8
ROW_BLOCK = 1024
WKV_CHUNK = 64
_ROW_BLOCKS = (1024, 512, 256, 128, 64, 32, 16, 8)


def _pick(n, candidates):
    for c in candidates:
        if c <= n and n % c == 0:
            return c
    return n


def _cparams(sem):
    return pltpu.CompilerParams(dimension_semantics=sem, vmem_limit_bytes=VMEM_LIMIT_BYTES)


def _mm_kernel(x_ref, w_ref, o_ref):
    o_ref[0] = jnp.dot(x_ref[0].astype(BF16), w_ref[0],
                       preferred_element_type=F32).astype(o_ref.dtype)


def _mm_res_kernel(x_ref, w_ref, r_ref, o_ref):
    o_ref[0] = r_ref[0] + jnp.dot(x_ref[0].astype(BF16), w_ref[0], preferred_element_type=F32)


def bmm(x, w, out_dtype=F32, res=None):
    g, m, k = x.shape
    n = w.shape[2]
    bm = _pick(m, (1024, 512, 256, 128, 64, 32, 16, 8))
    bn = n if n <= 1024 else _pick(n, (512, 256, 128))
    out_spec = pl.BlockSpec((1, bm, bn), lambda a, i, j: (a, i, j))
    in_specs = [pl.BlockSpec((1, bm, k), lambda a, i, j: (a, i, 0)),
                pl.BlockSpec((1, k, bn), lambda a, i, j: (a, 0, j))]
    return pl.pallas_call(
        _mm_kernel if res is None else _mm_res_kernel,
        grid=(g, m // bm, n // bn),
        in_specs=in_specs if res is None else in_specs + [out_spec],
        out_specs=out_spec,
        out_shape=jax.ShapeDtypeStruct((g, m, n), out_dtype),
        compiler_params=_cparams(("parallel", "parallel", "arbitrary")),
        name="bmm",
    )(*((x, w) if res is None else (x, w, res)))


def mm(x, w, out_dtype=F32, res=None):
    lead = x.shape[:-1]
    n = w.shape[1]
    out = bmm(x.reshape((1, -1, x.shape[-1])), w[None], out_dtype,
              None if res is None else res.reshape((1, -1, n)))
    return out.reshape(lead + (n,))


def _hmm_kernel(x_ref, w_ref, o_ref):
    o_ref[...] = jnp.dot(x_ref[...].astype(BF16), w_ref[0], preferred_element_type=F32).astype(o_ref.dtype)


def hmm(x, w, out_dtype=F32):
    m = x.shape[0]
    n_heads, k, n = w.shape
    bm = _pick(m, _ROW_BLOCKS)
    return pl.pallas_call(
        _hmm_kernel,
        grid=(n_heads, m // bm),
        in_specs=[pl.BlockSpec((bm, k), lambda h, i: (i, h)),
                  pl.BlockSpec((1, k, n), lambda h, i: (h, 0, 0))],
        out_specs=pl.BlockSpec((bm, n), lambda h, i: (i, h)),
        out_shape=jax.ShapeDtypeStruct((m, n_heads * n), out_dtype),
        compiler_params=_cparams(("parallel", "parallel")),
        name="hmm",
    )(x, w)


def _ffn_in_kernel(*refs, t_len, bm, seq_in_block):
    if seq_in_block:
        x_ref, wg_ref, wu_ref, cw_ref, cb_ref, e_ref, hid_ref, tail_ref = refs
    else:
        x_ref, xp_ref, wg_ref, wu_ref, cw_ref, cb_ref, e_ref, hid_ref, tail_ref = refs
    x = x_ref[...]
    wg = wg_ref[...]
    u = jnp.dot(x, wg, preferred_element_type=F32)
    up = jnp.dot(x, wu_ref[...], preferred_element_type=F32)
    row = lax.broadcasted_iota(jnp.int32, u.shape, 0)
    if seq_in_block:
        t = row % t_len
        u1 = jnp.where(t < 1, e_ref[0], pltpu.roll(u, 1, axis=0))
        u2 = jnp.where(t < 2, e_ref[1], pltpu.roll(u, 2, axis=0))
        tail_ref[...] = u
    else:
        prev8 = jnp.dot(xp_ref[...], wg, preferred_element_type=F32)
        seq_start = (pl.program_id(0) * bm) % t_len == 0
        prev8 = jnp.where(seq_start, e_ref[0], prev8)
        u1 = jnp.where(row < 1, prev8[7:8], pltpu.roll(u, 1, axis=0))
        u2 = jnp.where(row < 1, prev8[6:7], jnp.where(row < 2, prev8[7:8], pltpu.roll(u, 2, axis=0)))
        tail_ref[0] = u[bm - 8:bm]
    conv = cb_ref[...] + cw_ref[0:1] * u2 + cw_ref[1:2] * u1 + cw_ref[2:3] * u
    hid_ref[...] = (conv * jax.nn.sigmoid(conv) * up).astype(hid_ref.dtype)


def ffn_in(h, prev, w_gate, w_up, conv_w, conv_b):
    bsz, t_len, d = h.shape
    f = w_gate.shape[1]
    m = bsz * t_len
    x = h.reshape(m, d)
    bn = _pick(f, (512, 256, 128))
    seq_in_block = t_len < ROW_BLOCK
    if seq_in_block:
        bm = _pick(m, tuple(c for c in _ROW_BLOCKS if c <= ROW_BLOCK and c % t_len == 0))
        tpos = jnp.arange(t_len)[None, :, None]
        e = jnp.stack([jnp.broadcast_to(prev[:, 1:2], (bsz, t_len, f)),
                       jnp.where(tpos == 0, prev[:, 0:1], prev[:, 1:2])]).reshape(2, m, f)
        lead = []
        lead_specs = []
        e_spec = pl.BlockSpec((2, bm, bn), lambda i, j: (0, i, j))
        tail_shape = jax.ShapeDtypeStruct((m, f), F32)
        tail_spec = pl.BlockSpec((bm, bn), lambda i, j: (i, j))
    else:
        bm = _pick(t_len, tuple(c for c in _ROW_BLOCKS if c <= ROW_BLOCK))
        e = jnp.pad(prev, ((0, 0), (6, 0), (0, 0)))
        lead = [x]
        lead_specs = [pl.BlockSpec((8, d), lambda i, j: (jnp.maximum(i * (bm // 8) - 1, 0), 0))]
        e_spec = pl.BlockSpec((1, 8, bn), lambda i, j: ((i * bm) // t_len, 0, j))
        tail_shape = jax.ShapeDtypeStruct((m // bm, 8, f), F32)
        tail_spec = pl.BlockSpec((1, 8, bn), lambda i, j: (i, 0, j))
    w_spec = pl.BlockSpec((d, bn), lambda i, j: (0, j))
    hid, tail = pl.pallas_call(
        functools.partial(_ffn_in_kernel, t_len=t_len, bm=bm, seq_in_block=seq_in_block),
        grid=(m // bm, f // bn),
        in_specs=([pl.BlockSpec((bm, d), lambda i, j: (i, 0))] + lead_specs
                  + [w_spec, w_spec, pl.BlockSpec((CONV_W, bn), lambda i, j: (0, j)),
                     pl.BlockSpec((1, bn), lambda i, j: (0, j)), e_spec]),
        out_specs=[pl.BlockSpec((bm, bn), lambda i, j: (i, j)), tail_spec],
        out_shape=[jax.ShapeDtypeStruct((m, f), BF16), tail_shape],
        compiler_params=_cparams(("parallel", "arbitrary")),
        name="ffn_in",
    )(x, *lead, w_gate, w_up, conv_w, conv_b.reshape(1, f), e)
    if seq_in_block:
        state = tail.reshape(bsz, t_len, f)[:, t_len - 2:]
    else:
        state = tail.reshape(bsz, t_len // bm, 8, f)[:, -1, 6:]
    return hid, state


def _dot(a, b, dims=((1,), (0,))):
    return lax.dot_general(a.astype(BF16), b.astype(BF16), (dims, ((), ())), preferred_element_type=F32)


_NT = ((1,), (1,))
_TN = ((0,), (0,))


def _wkv_chunk_kernel(r_ref, ld_ref, k_ref, v_ref, kk_ref, as_ref, s0_ref, y_ref, sout_ref, s_sc, *,
                      c, hb, n, ng, nb):
    w = hb * n
    rb = hb * c
    rows = nb * rb
    first = pl.program_id(2) == 0
    last = pl.program_id(2) == pl.num_programs(2) - 1
    ri = lax.broadcasted_iota(jnp.int32, (rb, w), 0)
    ci = lax.broadcasted_iota(jnp.int32, (rb, w), 1)
    head_mask = (ri // c) == (ci // n)
    bd = lambda z: jnp.where(head_mask, jnp.concatenate([z] * hb, axis=0), 0.0)
    trow = lax.broadcasted_iota(jnp.int32, (c, w), 0)
    tr = lax.broadcasted_iota(jnp.int32, (rows, rows), 0)
    tc_ = lax.broadcasted_iota(jnp.int32, (rows, rows), 1)
    same = (tr // c) == (tc_ // c)
    strict = same & ((tr % c) > (tc_ % c))
    incl = same & ((tr % c) >= (tc_ % c))
    eye = jnp.where(tr == tc_, 1.0, 0.0)
    grp = range(ng)
    bat = range(nb)
    lanes = [slice(gi * w, (gi + 1) * w) for gi in grp]
    cat = lambda parts: parts[0] if len(parts) == 1 else jnp.concatenate(parts, axis=0)
    rsl = lambda z, bi: z[bi * rb:(bi + 1) * rb]

    @pl.when(first)
    def _():
        sr = lax.broadcasted_iota(jnp.int32, (w, w), 0)
        sc_ = lax.broadcasted_iota(jnp.int32, (w, w), 1)
        for gi in grp:
            for bi in bat:
                s0 = jnp.concatenate([s0_ref[bi, gi * hb + h] for h in range(hb)], axis=0)
                s_sc[gi * nb + bi] = jnp.where((sr // n) == (sc_ // n),
                                               jnp.concatenate([s0] * hb, axis=1), 0.0)

    def prep(gi):
        parts = []
        for bi in bat:
            r, ld, k, v, kk, asig = (ref[bi, :, lanes[gi]]
                                     for ref in (r_ref, ld_ref, k_ref, v_ref, kk_ref, as_ref))
            lam = ld
            shift = 1
            while shift < c:
                lam = lam + jnp.where(trow >= shift, pltpu.roll(lam, shift, axis=0), 0.0)
                shift *= 2
            g_inv = jnp.exp(-lam)
            parts.append((bd(-kk * jnp.exp(lam - ld)), bd(r * jnp.exp(lam)), bd(kk * asig * g_inv),
                          bd(k * g_inv), bd(v), jnp.exp(lam[c - 1:c])))
        return tuple(cat([pt[i] for pt in parts]) for i in range(5)) + ([pt[5] for pt in parts],)

    a_bd, r_bd, b_bd, k_bd, v_bd, g_end = zip(*[prep(gi) for gi in grp])
    p = [_dot(jnp.concatenate([a_bd[gi], r_bd[gi]], axis=0),
              jnp.concatenate([b_bd[gi], k_bd[gi]], axis=0), _NT) for gi in grp]
    l_pow = [jnp.where(strict, p[gi][:rows, :rows], 0.0) for gi in grp]
    t_inv = [eye + l_pow[gi] for gi in grp]
    for _ in range(c.bit_length() - 2):
        l_pow = [_dot(l_pow[gi], l_pow[gi]) for gi in grp]
        t_inv = [t_inv[gi] + _dot(t_inv[gi], l_pow[gi]) for gi in grp]
    s = [[s_sc[gi * nb + bi] for bi in bat] for gi in grp]
    xs = [[_dot(jnp.concatenate([rsl(a_bd[gi], bi), rsl(r_bd[gi], bi)], axis=0), s[gi][bi], _NT)
           for bi in bat] for gi in grp]
    xs_a = [cat([xs[gi][bi][:rb] for bi in bat]) for gi in grp]
    xs_r = [cat([xs[gi][bi][rb:] for bi in bat]) for gi in grp]
    lkv = [_dot(jnp.where(strict, p[gi][:rows, rows:], 0.0), v_bd[gi]) for gi in grp]
    pkv = [_dot(jnp.where(incl, p[gi][rows:, rows:], 0.0), v_bd[gi]) for gi in grp]
    u = [_dot(t_inv[gi], xs_a[gi] + lkv[gi]) for gi in grp]
    y_bd = [xs_r[gi] + pkv[gi] + _dot(jnp.where(incl, p[gi][rows:, :rows], 0.0), u[gi]) for gi in grp]
    s_new = [[s[gi][bi] * g_end[gi][bi]
              + _dot(jnp.concatenate([rsl(u[gi], bi), rsl(v_bd[gi], bi)], axis=0),
                     jnp.concatenate([rsl(b_bd[gi], bi), rsl(k_bd[gi], bi)], axis=0) * g_end[gi][bi], _TN)
              for bi in bat] for gi in grp]
    for gi in grp:
        for bi in bat:
            y = y_bd[gi][bi * rb:bi * rb + c]
            for h in range(1, hb):
                y = y + y_bd[gi][bi * rb + h * c:bi * rb + (h + 1) * c]
            y_ref[bi, :, lanes[gi]] = y
            s_sc[gi * nb + bi] = s_new[gi][bi]

    @pl.when(last)
    def _():
        for gi in grp:
            for bi in bat:
                for h in range(hb):
                    sout_ref[bi, gi * hb + h] = s_new[gi][bi][h * n:(h + 1) * n, h * n:(h + 1) * n]


def wkv_chunked(r, ld, k, v, kk, asig, s0, n_heads):
    bsz, t, d = r.shape
    n = d // n_heads
    hb = _pick(n_heads, (4, 2, 1))
    ng = _pick(n_heads // hb, (4, 2, 1))
    c = _pick(t, (WKV_CHUNK, 32, 16, 8))
    nb = _pick(bsz, tuple(x for x in (8, 4, 2, 1) if x * hb * c <= MXU_WIDTH))
    seq_spec = pl.BlockSpec((nb, c, ng * hb * n), lambda b, q, ch: (b, ch, q))
    st_spec = pl.BlockSpec((nb, ng * hb, n, n), lambda b, q, ch: (b, q, 0, 0))
    return pl.pallas_call(
        functools.partial(_wkv_chunk_kernel, c=c, hb=hb, n=n, ng=ng, nb=nb),
        grid=(bsz // nb, n_heads // (hb * ng), t // c),
        in_specs=[seq_spec] * 6 + [st_spec],
        out_specs=[seq_spec, st_spec],
        out_shape=[jax.ShapeDtypeStruct((bsz, t, d), F32),
                   jax.ShapeDtypeStruct((bsz, n_heads, n, n), F32)],
        scratch_shapes=[pltpu.VMEM((ng * nb, hb * n, hb * n), F32)],
        compiler_params=_cparams(("parallel", "parallel", "arbitrary")),
        name="wkv_chunked",
    )(r, ld, k, v, kk, asig, s0)


def _head_sum(x, n):
    wd = min(MXU_WIDTH, x.shape[1])
    gi = lax.broadcasted_iota(jnp.int32, (wd, wd), 0) // n
    gj = lax.broadcasted_iota(jnp.int32, (wd, wd), 1) // n
    g = jnp.where(gi == gj, 1.0, 0.0).astype(BF16)
    hi = x.astype(BF16)
    lo = (x - hi.astype(F32)).astype(BF16)
    parts = []
    for s in range(0, x.shape[1], wd):
        parts.append(jnp.dot(hi[:, s:s + wd], g, preferred_element_type=F32)
                     + jnp.dot(lo[:, s:s + wd], g, preferred_element_type=F32))
    return parts[0] if len(parts) == 1 else jnp.concatenate(parts, axis=1)


def _softplus(z):
    return jnp.maximum(z, 0.0) + jnp.log(1.0 + jnp.exp(-jnp.abs(z)))


def _rwkv_pre_kernel(*refs, n, has_v_lora):
    if has_v_lora:
        (k_ref, v_ref, hw_ref, ha_ref, w2_ref, a2_ref, w0_ref, a0_ref, kk_ref, ka_ref,
         vf_ref, hv_ref, v2_ref, v0_ref, ld_out, a_out, kk_out, k_out, v_out) = refs
    else:
        (k_ref, v_ref, hw_ref, ha_ref, w2_ref, a2_ref, w0_ref, a0_ref, kk_ref, ka_ref,
         ld_out, a_out, kk_out, k_out) = refs
    k = k_ref[...]
    w_pre = w0_ref[...] + jnp.dot(jnp.tanh(hw_ref[...]).astype(BF16), w2_ref[...], preferred_element_type=F32)
    ld_out[...] = -jnp.exp(-_softplus(-w_pre) - 0.5)
    a = jax.nn.sigmoid(a0_ref[...] + jnp.dot(ha_ref[...].astype(BF16), a2_ref[...], preferred_element_type=F32))
    a_out[...] = a
    kk = k * kk_ref[...]
    kk_out[...] = kk / jnp.maximum(jnp.sqrt(_head_sum(kk * kk, n)), 1e-12)
    k_out[...] = k * (1.0 + (a - 1.0) * ka_ref[...])
    if has_v_lora:
        v = v_ref[...]
        gate = jax.nn.sigmoid(v0_ref[...] + jnp.dot(hv_ref[...].astype(BF16), v2_ref[...],
                                                    preferred_element_type=F32))
        v_out[...] = v + (vf_ref[...] - v) * gate


def rwkv_pre(k, v, hw, ha, p, v_first, hv, n):
    m, d = k.shape
    bm = _pick(m, (256, 128, 64, 32, 16, 8))
    has_v_lora = p["v_lora"] is not None
    row = lambda width: pl.BlockSpec((bm, width), lambda i: (i, 0))
    full = lambda a: pl.BlockSpec(a.shape, lambda i: (0, 0))
    vec = lambda z: z.reshape(1, d)
    args = [k, v, hw, ha, p["w2"], p["a2"], vec(p["w0"]), vec(p["a0"]), vec(p["k_k"]), vec(p["k_a"])]
    specs = [row(d), row(d), row(hw.shape[1]), row(ha.shape[1])] + [full(a) for a in args[4:]]
    n_out = 4
    if has_v_lora:
        v0, _, v2 = p["v_lora"]
        extra = [v_first, hv, v2, vec(v0)]
        args += extra
        specs += [row(d), row(hv.shape[1]), full(v2), full(extra[3])]
        n_out = 5
    outs = pl.pallas_call(
        functools.partial(_rwkv_pre_kernel, n=n, has_v_lora=has_v_lora),
        grid=(m // bm,),
        in_specs=specs,
        out_specs=[row(d)] * n_out,
        out_shape=[jax.ShapeDtypeStruct((m, d), F32)] * n_out,
        compiler_params=_cparams(("parallel",)),
        name="rwkv_pre",
    )(*args)
    return tuple(outs) + ((v,) if not has_v_lora else ())


def _rwkv_post_kernel(y_ref, r_ref, k_ref, v_ref, hg_ref, g2_ref, lnw_ref, lnb_ref, rk_ref, o_ref, *, n):
    y = y_ref[...]
    mu = _head_sum(y, n) / n
    dlt = y - mu
    var = _head_sum(dlt * dlt, n) / n
    yn = dlt * lax.rsqrt(var + n * GN_EPS_PER_CHANNEL) * lnw_ref[...] + lnb_ref[...]
    bonus = _head_sum(r_ref[...] * k_ref[...] * rk_ref[...], n) * v_ref[...]
    g = jnp.dot(jax.nn.sigmoid(hg_ref[...]).astype(BF16), g2_ref[...], preferred_element_type=F32)
    o_ref[...] = ((yn + bonus) * g).astype(o_ref.dtype)


def rwkv_post(y, r, k, v, hg, p, n):
    m, d = y.shape
    bm = _pick(m, (256, 128, 64, 32, 16, 8))
    row = lambda width: pl.BlockSpec((bm, width), lambda i: (i, 0))
    full = lambda a: pl.BlockSpec(a.shape, lambda i: (0, 0))
    vec = lambda z: z.reshape(1, d)
    consts = [p["g2"], vec(p["ln_w"]), vec(p["ln_b"]), vec(p["r_k"])]
    return pl.pallas_call(
        functools.partial(_rwkv_post_kernel, n=n),
        grid=(m // bm,),
        in_specs=[row(d)] * 4 + [row(hg.shape[1])] + [full(a) for a in consts],
        out_specs=row(d),
        out_shape=jax.ShapeDtypeStruct((m, d), BF16),
        compiler_params=_cparams(("parallel",)),
        name="rwkv_post",
    )(y, r, k, v, hg, *consts)


def _flash_kernel(qn_ref, qr_ref, kn_ref, kr_ref, v_ref, o_ref, *, bq, n_q, scale):
    qi = pl.program_id(2)
    qn = qn_ref[0]
    qr = qr_ref[0]
    dn = (((1,), (1,)), ((), ()))

    for i in range(n_q):
        @pl.when(qi == i)
        def _(i=i):
            kv_len = (i + 1) * bq
            s = (lax.dot_general(qn, kn_ref[0, :kv_len, :], dn, preferred_element_type=F32)
                 + lax.dot_general(qr, kr_ref[0, :kv_len, :], dn, preferred_element_type=F32)) * scale
            qpos = i * bq + lax.broadcasted_iota(jnp.int32, (bq, kv_len), 0)
            kpos = lax.broadcasted_iota(jnp.int32, (bq, kv_len), 1)
            s = jnp.where(kpos <= qpos, s, -jnp.inf)
            p = jnp.exp(s - jnp.max(s, axis=-1, keepdims=True))
            l = jnp.sum(p, axis=-1, keepdims=True)
            o = jnp.dot(p.astype(BF16), v_ref[0, :kv_len, :], preferred_element_type=F32)
            o_ref[0] = (o / l).astype(o_ref.dtype)


def flash_prompt(qn, qr, kn, kr, v, n_heads, scale):
    bsz, s_len, _ = qn.shape
    dh = qn.shape[-1] // n_heads
    dv = v.shape[-1] // n_heads
    bq = _pick(s_len, (512, 256, 128, 64, 32, 16, 8))
    n_q = s_len // bq
    q_spec = pl.BlockSpec((1, bq, dh), lambda b, h, i: (b, i, h))
    k_spec = pl.BlockSpec((1, s_len, dh), lambda b, h, i: (b, 0, h))
    return pl.pallas_call(
        functools.partial(_flash_kernel, bq=bq, n_q=n_q, scale=scale),
        grid=(bsz, n_heads, n_q),
        in_specs=[q_spec, q_spec, k_spec, k_spec,
                  pl.BlockSpec((1, s_len, dv), lambda b, h, i: (b, 0, h))],
        out_specs=pl.BlockSpec((1, bq, dv), lambda b, h, i: (b, i, h)),
        out_shape=jax.ShapeDtypeStruct((bsz, s_len, n_heads * dv), BF16),
        compiler_params=_cparams(("parallel", "parallel", "arbitrary")),
        name="flash_prompt",
    )(qn, qr, kn, kr, v)


def _mla_head_kernel(*refs, n_heads, d_qk, half, is_query):
    tile = lambda z: jnp.concatenate([z] * n_heads, axis=1)
    if is_query:
        xn_ref, xr_ref, cos_ref, sin_a_ref, sin_b_ref, gn_ref, gr_ref, on_ref, or_ref = refs
        xr = xr_ref[...]
        width = xr.shape[1]
        xr = (xr * tile(cos_ref[...]) + pltpu.roll(xr, width - half, axis=1) * tile(sin_a_ref[...])
              + pltpu.roll(xr, half, axis=1) * tile(sin_b_ref[...]))
    else:
        xn_ref, kpe_ref, gn_ref, gr_ref, on_ref, or_ref = refs
        kpe = kpe_ref[...]
        pad = jnp.zeros((kpe.shape[0], xn_ref.shape[1] // n_heads - kpe.shape[1]), F32)
        xr = tile(jnp.concatenate([kpe, pad], axis=1))
    xn = xn_ref[...]
    inv = lax.rsqrt(_head_sum(xn * xn + xr * xr, xn.shape[1] // n_heads) / d_qk + RMS_EPS)
    on_ref[...] = (xn * inv * gn_ref[...]).astype(on_ref.dtype)
    or_ref[...] = (xr * inv * gr_ref[...]).astype(or_ref.dtype)


def mla_head_norm(x, gain_n, gain_r, n_heads, d_qk, d_rope, rope_tabs=None, k_pe=None):
    m = x.shape[0]
    is_query = rope_tabs is not None
    wd = gain_n.shape[1]
    dh = wd // n_heads
    bm = _pick(m, (256, 128, 64, 32, 16, 8))
    row = lambda width, blk=0: pl.BlockSpec((bm, width), lambda i, blk=blk: (i, blk))
    full = lambda a: pl.BlockSpec(a.shape, lambda i: (0, 0))
    if is_query:
        args = [x, x, *rope_tabs, gain_n, gain_r]
        specs = [row(wd, 0), row(wd, 1), row(dh), row(dh), row(dh), full(gain_n), full(gain_r)]
    else:
        args = [x, k_pe, gain_n, gain_r]
        specs = [row(wd), row(k_pe.shape[1]), full(gain_n), full(gain_r)]
    return pl.pallas_call(
        functools.partial(_mla_head_kernel, n_heads=n_heads, d_qk=d_qk, half=d_rope // 2, is_query=is_query),
        grid=(m // bm,),
        in_specs=specs,
        out_specs=[row(wd), row(wd)],
        out_shape=[jax.ShapeDtypeStruct((m, wd), BF16)] * 2,
        compiler_params=_cparams(("parallel",)),
        name="mla_head_norm",
    )(*args)


def _sum_rows_to_lanes(x):
    hi = x.astype(BF16)
    lo = (x - hi.astype(F32)).astype(BF16)
    ones = jnp.ones((8, x.shape[1]), BF16)
    dn = (((1,), (1,)), ((), ()))
    out = (lax.dot_general(ones, hi, dn, preferred_element_type=F32)
           + lax.dot_general(ones, lo, dn, preferred_element_type=F32))
    return out[0:1]


def _key_scale_block(wt, c, kpe, n_heads, d_qk):
    k_t = lax.dot_general(wt, c.astype(BF16), (((1,), (1,)), ((), ())), preferred_element_type=F32)
    sq = k_t * k_t
    ss = jnp.sum(sq.reshape(n_heads, sq.shape[0] // n_heads, sq.shape[1]), axis=1)
    kpe = kpe.astype(F32)
    pe = _sum_rows_to_lanes(kpe * kpe)
    return lax.rsqrt((ss + pe) / d_qk + RMS_EPS)


def _kscale_kernel(pt_ref, *refs, npg, n_heads, d_qk):
    del pt_ref
    c_refs = refs[:npg]
    kpe_refs = refs[npg:2 * npg]
    wt_ref = refs[2 * npg]
    s_ref, ckv_ref = refs[2 * npg + 1:]
    page, d_c = c_refs[0].shape[1:]
    d_r = kpe_refs[0].shape[2]
    pair = 2 if npg % 2 == 0 else 1
    for p in range(0, npg, pair):
        rows = slice(p * page, (p + pair) * page)
        c = jnp.concatenate([c_refs[p + q][0] for q in range(pair)], axis=0)
        kpe = jnp.concatenate([kpe_refs[p + q][0] for q in range(pair)], axis=0)
        s_ref[0, :, rows] = _key_scale_block(wt_ref[...], c, kpe, n_heads, d_qk)
        ckv_ref[0, rows, 0:d_c] = c.astype(BF16)
        ckv_ref[0, rows, d_c:d_c + d_r] = kpe.astype(BF16)


def paged_key_scale(cache_c, cache_kpe, page_table_flat, w_uk_t, n_heads, d_qk, npg):
    n_used = page_table_flat.shape[0]
    page, d_c = cache_c.shape[1:]
    d_r = cache_kpe.shape[2]
    n_grp = n_used // npg

    def page_spec(width, p):
        return pl.BlockSpec((1, page, width), lambda g, pt, p=p: (pt[g * npg + p], 0, 0))

    in_specs = ([page_spec(d_c, p) for p in range(npg)] + [page_spec(d_r, p) for p in range(npg)]
                + [pl.BlockSpec(w_uk_t.shape, lambda g, pt: (0, 0))])
    grid_spec = pltpu.PrefetchScalarGridSpec(
        num_scalar_prefetch=1, grid=(n_grp,), in_specs=in_specs,
        out_specs=[pl.BlockSpec((1, n_heads, npg * page), lambda g, pt: (g, 0, 0)),
                   pl.BlockSpec((1, npg * page, d_c + d_r), lambda g, pt: (g, 0, 0))])
    return pl.pallas_call(
        functools.partial(_kscale_kernel, npg=npg, n_heads=n_heads, d_qk=d_qk),
        grid_spec=grid_spec,
        out_shape=[jax.ShapeDtypeStruct((n_grp, n_heads, npg * page), F32),
                   jax.ShapeDtypeStruct((n_grp, npg * page, d_c + d_r), BF16)],
        compiler_params=_cparams(("arbitrary",)),
        name="paged_key_scale",
    )(page_table_flat, *([cache_c] * npg), *([cache_kpe] * npg), w_uk_t)


def _paged_attn_kernel(q_ref, new_ref, snew_ref, ckv_ref, s_ref, o_ref, m_sc, l_sc, acc_sc, *,
                       bb, n_heads, t_len, d_c, scale):
    g = pl.program_id(1)
    rows = n_heads * t_len
    dn = (((1,), (1,)), ((), ()))

    def process(ckv, key_scale, mask):
        bs = range(bb)
        sc = [lax.dot_general(q_ref[b], ckv[b], dn, preferred_element_type=F32) for b in bs]
        sc = [sc[b] * (jnp.concatenate([key_scale[b]] * t_len, axis=0) * scale) for b in bs]
        if mask is not None:
            sc = [jnp.where(mask, sc[b], -jnp.inf) for b in bs]
        m_new = [jnp.maximum(m_sc[b], jnp.max(sc[b], axis=-1, keepdims=True)) for b in bs]
        alpha = [jnp.exp(m_sc[b] - m_new[b]) for b in bs]
        p = [jnp.exp(sc[b] - m_new[b]) for b in bs]
        pv = [jnp.dot(p[b].astype(BF16), ckv[b][:, :d_c], preferred_element_type=F32) for b in bs]
        for b in bs:
            l_sc[b] = l_sc[b] * alpha[b] + jnp.sum(p[b], axis=-1, keepdims=True)
            acc_sc[b] = acc_sc[b] * alpha[b] + pv[b]
            m_sc[b] = m_new[b]

    @pl.when(g == 0)
    def _():
        m_sc[...] = jnp.full(m_sc.shape, -jnp.inf, F32)
        l_sc[...] = jnp.zeros(l_sc.shape, F32)
        acc_sc[...] = jnp.zeros(acc_sc.shape, F32)
        nk = new_ref.shape[1]
        t_row = lax.broadcasted_iota(jnp.int32, (rows, nk), 0) // n_heads
        key = lax.broadcasted_iota(jnp.int32, (rows, nk), 1)
        process([new_ref[b] for b in range(bb)], [snew_ref[b] for b in range(bb)], key <= t_row)

    process([ckv_ref[b, 0] for b in range(bb)], [s_ref[b, 0] for b in range(bb)], None)

    @pl.when(g == pl.num_programs(1) - 1)
    def _():
        for b in range(bb):
            o_ref[b] = acc_sc[b] / l_sc[b]


def paged_attention(q_cat, ckv_new, s_new, ckv, s_past, n_heads, t_len, d_c, scale):
    bsz, rows, d_cat = q_cat.shape
    _, n_grp, nk, _ = ckv.shape
    kn = ckv_new.shape[1]
    bb = _pick(bsz, (4, 2, 1))

    def per_b(shape):
        return pl.BlockSpec((bb,) + shape, lambda b, g: (b, 0, 0))

    return pl.pallas_call(
        functools.partial(_paged_attn_kernel, bb=bb, n_heads=n_heads, t_len=t_len, d_c=d_c, scale=scale),
        grid=(bsz // bb, n_grp),
        in_specs=[per_b((rows, d_cat)), per_b((kn, d_cat)), per_b((n_heads, kn)),
                  pl.BlockSpec((bb, 1, nk, d_cat), lambda b, g: (b, g, 0, 0)),
                  pl.BlockSpec((bb, 1, n_heads, nk), lambda b, g: (b, g, 0, 0))],
        out_specs=per_b((rows, d_c)),
        out_shape=jax.ShapeDtypeStruct((bsz, rows, d_c), F32),
        scratch_shapes=[pltpu.VMEM((bb, rows, 1), F32), pltpu.VMEM((bb, rows, 1), F32),
                        pltpu.VMEM((bb, rows, d_c), F32)],
        compiler_params=_cparams(("parallel", "arbitrary")),
        name="paged_attention",
    )(q_cat, ckv_new, s_new, ckv, s_past)


def _rms_norm(x, g):
    return x * lax.rsqrt(jnp.mean(x * x, axis=-1, keepdims=True) + RMS_EPS) * g


def _rope(x, pos):
    half = x.shape[-1] // 2
    inv = ROPE_BASE ** (-jnp.arange(half, dtype=F32) / half)
    ang = pos.astype(F32)[:, None] * inv[None, :]
    cos = jnp.cos(ang)[None, :, None, :]
    sin = jnp.sin(ang)[None, :, None, :]
    x1, x2 = x[..., :half], x[..., half:]
    return jnp.concatenate([x1 * cos - x2 * sin, x1 * sin + x2 * cos], axis=-1)


def _conv_ffn(x, prev, p):
    h = _rms_norm(x, p["norm"]).astype(BF16)
    hid, state = ffn_in(h, prev, p["w_gate"], p["w_up"], p["conv_w"], p["conv_b"])
    return mm(hid, p["w_down"], res=x.reshape(hid.shape[0], -1)).reshape(x.shape), state


def _rwkv7_time_mix(x, shift_prev, wkv_prev, v_first, p):
    bsz, t, d = x.shape
    n_heads, n = p["r_k"].shape
    h = _rms_norm(x, p["norm"])
    prev = jnp.concatenate([shift_prev[:, None, :], h[:, :-1]], axis=1)
    xx = prev - h
    xr, xw, xk, xv, xa, xg = ((h + xx * p["mix"][i]).astype(BF16) for i in range(6))
    flat = lambda z: z.reshape(bsz * t, -1)
    r = mm(flat(xr), p["wr"])
    k = mm(flat(xk), p["wk"])
    v = mm(flat(xv), p["wv"])
    hv = None if p["v_lora"] is None else mm(flat(xv), p["v_lora"][1])
    log_decay, a, kk, k, v = rwkv_pre(k, v, mm(flat(xw), p["w1"]), mm(flat(xa), p["a1"]), p, v_first, hv, n)
    if p["v_lora"] is None:
        v_first = v
    seq = lambda z: z.reshape(bsz, t, d)
    y, s = wkv_chunked(seq(r), seq(log_decay), seq(k), seq(v), seq(kk), seq(a), wkv_prev, n_heads)
    out = rwkv_post(flat(y), r, k, v, mm(flat(xg), p["g1"]), p, n)
    return mm(out, p["wo"], res=flat(x)).reshape(x.shape), h[:, -1], s, v_first


def _mla_kv_side(x, pos, kv):
    u = _rms_norm(x, kv["norm"]).astype(BF16)
    d_c = kv["norm_c"].shape[0]
    ckr = mm(u, kv["w_dkv_kr"])
    c = _rms_norm(ckr[..., :d_c], kv["norm_c"])
    k_pe = _rope(ckr[..., d_c:][:, :, None, :], pos)[:, :, 0, :]
    return c, k_pe


def _head_gains(g, n_heads, d_nope, other=None):
    if other is not None:
        g = g * other
    g_r = jnp.pad(g[d_nope:], (0, 2 * d_nope - g.shape[0]))
    return jnp.tile(g[:d_nope], n_heads)[None], jnp.tile(g_r, n_heads)[None]


def _mla_query(x, pos, p, kv, n_heads, d_nope, d_rope, absorb_key_gain):
    bsz, t, _ = x.shape
    h = _rms_norm(x, p["norm"]).astype(BF16)
    cq = _rms_norm(mm(h, p["w_dq"]), p["norm_cq"]).astype(BF16)
    q = mm(cq.reshape(bsz * t, -1), p["w_uq"])
    half = d_rope // 2
    inv = ROPE_BASE ** (-jnp.arange(half, dtype=F32) / half)
    ang = pos.astype(F32)[:, None] * inv[None, :]
    cos, sin, zero = jnp.cos(ang), jnp.sin(ang), jnp.zeros((t, half), F32)
    lane_pad = jnp.zeros((t, d_nope - d_rope), F32)
    tabs = tuple(jnp.tile(jnp.concatenate(parts + [lane_pad], axis=1), (bsz, 1))
                 for parts in ([cos, cos], [-sin, zero], [zero, sin]))
    g_n, g_r = _head_gains(p["norm_q"], n_heads, d_nope, kv["norm_k"] if absorb_key_gain else None)
    return mla_head_norm(q, g_n, g_r, n_heads, d_nope + d_rope, d_rope, rope_tabs=tabs)


def _prompt_attend_fn(c, k_pe, kv, n_heads, d_nope):
    bsz, s_len, _ = c.shape
    d_rope = k_pe.shape[-1]
    d_qk = d_nope + d_rope
    g_n, g_r = _head_gains(kv["norm_k"], n_heads, d_nope)
    k_n, k_r = mla_head_norm(mm(c.reshape(bsz * s_len, -1), kv["w_uk_2d"]), g_n, g_r, n_heads, d_qk, d_rope,
                             k_pe=k_pe.reshape(bsz * s_len, d_rope))
    v = mm(c, kv["w_uv_2d"], BF16)
    seq = lambda z: z.reshape(bsz, s_len, -1)

    def attend(q):
        return flash_prompt(seq(q[0]), seq(q[1]), seq(k_n), seq(k_r), v, n_heads, d_qk ** -0.5)

    return attend


def _sample_attend_fn(c_new, kpe_new, cache_c, cache_kpe, page_table, kv, n_heads, d_nope):
    bsz, t, d_c = c_new.shape
    d_r = kpe_new.shape[-1]
    d_qk = d_nope + d_r
    page = cache_c.shape[1]
    pt_flat = page_table.reshape(-1)
    npg = _pick(page_table.shape[1], (PAGES_PER_STEP, 4, 2, 1))

    k_nope_new = mm(c_new, kv["w_uk_2d"]).reshape(bsz, t, n_heads, d_nope)
    ss = jnp.sum(k_nope_new * k_nope_new, axis=-1) + jnp.sum(kpe_new * kpe_new, axis=-1)[..., None]
    s_new = lax.rsqrt(ss / d_qk + RMS_EPS)
    kn = page
    ckv_new = jnp.pad(jnp.concatenate([c_new, kpe_new], axis=-1).astype(BF16), ((0, 0), (0, kn - t), (0, 0)))
    s_new_p = jnp.pad(s_new.transpose(0, 2, 1), ((0, 0), (0, 0), (0, kn - t)), constant_values=1.0)

    s_past, ckv = paged_key_scale(cache_c, cache_kpe, pt_flat, kv["w_uk_t"], n_heads, d_qk, npg)
    n_grp = page_table.shape[1] // npg
    s_past = s_past.reshape(bsz, n_grp, n_heads, npg * page)
    ckv = ckv.reshape(bsz, n_grp, npg * page, d_c + d_r)
    scale = d_qk ** -0.5

    def attend(q):
        q_lat = hmm(q[0], kv["w_uk_hdc"], BF16).reshape(bsz, t * n_heads, d_c)
        q_r = q[1].reshape(bsz, t * n_heads, d_nope)[..., :d_r]
        q_cat = jnp.concatenate([q_lat, q_r], axis=-1)
        o_lat = paged_attention(q_cat, ckv_new, s_new_p, ckv, s_past, n_heads, t, d_c, scale)
        o = hmm(o_lat.reshape(bsz * t, n_heads * d_c), kv["w_uv_hcd"])
        return o.reshape(bsz, t, -1)

    return attend


def kernel(x_prompt, x_sample, cache_kv_latent, cache_k_pe, state_wkv, state_shift, state_conv, page_table,
           ffn_norm, ffn_w_gate, ffn_w_up, ffn_conv_w, ffn_conv_b, ffn_w_down,
           rw_norm, rw_mix, rw_w0, rw_w1, rw_w2, rw_a0, rw_a1, rw_a2, rw_v0, rw_v1, rw_v2,
           rw_g1, rw_g2, rw_k_k, rw_k_a, rw_r_k, rw_wr, rw_wk, rw_wv, rw_ln_w, rw_ln_b, rw_wo,
           kv_norm, kv_w_dkv, kv_norm_c, kv_w_kr, kv_w_uk, kv_w_uv, kv_norm_k,
           mla_norm, mla_w_dq, mla_norm_cq, mla_w_uq, mla_norm_q, mla_wo):
    depth = ffn_norm.shape[0]
    n_a = rw_norm.shape[0]
    d_c, n_heads, d_nope = kv_w_uk.shape
    d_v = kv_w_uv.shape[2]
    bf = lambda z: z.astype(BF16)

    ffn = [dict(norm=ffn_norm[l], w_gate=bf(ffn_w_gate[l]), w_up=bf(ffn_w_up[l]), conv_w=ffn_conv_w[l],
                conv_b=ffn_conv_b[l], w_down=bf(ffn_w_down[l])) for l in range(depth)]
    rw = [dict(norm=rw_norm[l], mix=rw_mix[l], w0=rw_w0[l], w1=bf(rw_w1[l]), w2=bf(rw_w2[l]),
               a0=rw_a0[l], a1=bf(rw_a1[l]), a2=bf(rw_a2[l]), g1=bf(rw_g1[l]), g2=bf(rw_g2[l]),
               k_k=rw_k_k[l], k_a=rw_k_a[l], r_k=rw_r_k[l], wr=bf(rw_wr[l]), wk=bf(rw_wk[l]),
               wv=bf(rw_wv[l]), ln_w=rw_ln_w[l], ln_b=rw_ln_b[l], wo=bf(rw_wo[l]),
               v_lora=None if l == 0 else (rw_v0[l - 1], bf(rw_v1[l - 1]), bf(rw_v2[l - 1])))
          for l in range(n_a)]
    kv = dict(norm=kv_norm, norm_c=kv_norm_c, norm_k=kv_norm_k,
              w_dkv_kr=bf(jnp.concatenate([kv_w_dkv, kv_w_kr], axis=1)),
              w_uk_2d=bf(kv_w_uk.reshape(d_c, n_heads * d_nope)),
              w_uv_2d=bf(kv_w_uv.reshape(d_c, n_heads * d_v)),
              w_uk_t=bf(kv_w_uk.reshape(d_c, n_heads * d_nope).T),
              w_uk_hdc=bf(kv_w_uk.transpose(1, 2, 0)),
              w_uv_hcd=bf(kv_w_uv.transpose(1, 0, 2)))
    d_qc = mla_w_uq.shape[1]
    d_rope = kv_w_kr.shape[1]

    def split_heads(w_uq):
        w_r = jnp.pad(w_uq[:, :, d_nope:], ((0, 0), (0, 0), (0, d_nope - d_rope)))
        return jnp.concatenate([w_uq[:, :, :d_nope].reshape(d_qc, -1), w_r.reshape(d_qc, -1)], axis=1)

    mla = [dict(norm=mla_norm[b], w_dq=bf(mla_w_dq[b]), norm_cq=mla_norm_cq[b],
                w_uq=bf(split_heads(mla_w_uq[b])), norm_q=mla_norm_q[b],
                wo=bf(mla_wo[b])) for b in range(depth - n_a)]

    def trunk(x, pos, shift_in, wkv_in, conv_in, make_attend, absorb_key_gain):
        v_first = attend = c = k_pe = None
        shifts, wkvs, convs = [], [], []
        for layer in range(depth):
            if layer < n_a:
                x, sh, s, v_first = _rwkv7_time_mix(x, shift_in[layer], wkv_in[layer], v_first, rw[layer])
                shifts.append(sh)
                wkvs.append(s)
            else:
                if layer == n_a:
                    c, k_pe = _mla_kv_side(x, pos, kv)
                    attend = make_attend(c, k_pe)
                p = mla[layer - n_a]
                q = _mla_query(x, pos, p, kv, n_heads, d_nope, d_rope, absorb_key_gain)
                x = mm(attend(q), p["wo"], res=x)
            x, cv = _conv_ffn(x, conv_in[layer], ffn[layer])
            convs.append(cv)
        return x, c, k_pe, jnp.stack(wkvs), jnp.stack(shifts), jnp.stack(convs)

    bp, sp, d = x_prompt.shape
    n_rw_heads, n = rw_r_k.shape[1:]
    d_ff = ffn_w_gate.shape[2]
    out_p = trunk(
        x_prompt, jnp.arange(sp),
        jnp.zeros((n_a, bp, d), F32), jnp.zeros((n_a, bp, n_rw_heads, n, n), F32),
        jnp.zeros((depth, bp, CONV_W - 1, d_ff), F32),
        lambda c, kp: _prompt_attend_fn(c, kp, kv, n_heads, d_nope), False)

    past_len = page_table.shape[1] * cache_kv_latent.shape[1]
    out_s = trunk(
        x_sample, past_len + jnp.arange(x_sample.shape[1]), state_shift, state_wkv, state_conv,
        lambda c, kp: _sample_attend_fn(c, kp, cache_kv_latent, cache_k_pe, page_table, kv, n_heads, d_nope),
        True)

    return (out_p[0], out_s[0]) + tuple(out_p[1:]) + tuple(out_s[1:])
```

```python
import functools

import jax
import jax.numpy as jnp
from jax import lax
from jax.experimental import pallas as pl
from jax.experimental.pallas import tpu as pltpu

F32 = jnp.float32
BF16 = jnp.bfloat16

RMS_EPS = 1e-6
ROPE_BASE = 10000.0
GN_EPS_PER_CHANNEL = 1e-5
CONV_W = 3

VMEM_LIMIT_BYTES = 52 * 1024 * 1024
LANES = 128
MXU_WIDTH = 256
PAGES_PER_STEP = 16
ROW_BLOCK = 1024
WKV_CHUNK = 64
_ROW_BLOCKS = (1024, 512, 256, 128, 64, 32, 16, 8)


def _pick(n, candidates):
    for c in candidates:
        if c <= n and n % c == 0:
            return c
    return n


def _cparams(sem):
    return pltpu.CompilerParams(dimension_semantics=sem, vmem_limit_bytes=VMEM_LIMIT_BYTES)


def _mm_kernel(x_ref, w_ref, o_ref):
    o_ref[0] = jnp.dot(x_ref[0].astype(BF16), w_ref[0],
                       preferred_element_type=F32).astype(o_ref.dtype)


def _mm_res_kernel(x_ref, w_ref, r_ref, o_ref):
    o_ref[0] = r_ref[0] + jnp.dot(x_ref[0].astype(BF16), w_ref[0], preferred_element_type=F32)


def bmm(x, w, out_dtype=F32, res=None):
    g, m, k = x.shape
    n = w.shape[2]
    bm = _pick(m, (1024, 512, 256, 128, 64, 32, 16, 8))
    bn = n if n <= 1024 else _pick(n, (512, 256, 128))
    out_spec = pl.BlockSpec((1, bm, bn), lambda a, i, j: (a, i, j))
    in_specs = [pl.BlockSpec((1, bm, k), lambda a, i, j: (a, i, 0)),
                pl.BlockSpec((1, k, bn), lambda a, i, j: (a, 0, j))]
    return pl.pallas_call(
        _mm_kernel if res is None else _mm_res_kernel,
        grid=(g, m // bm, n // bn),
        in_specs=in_specs if res is None else in_specs + [out_spec],
        out_specs=out_spec,
        out_shape=jax.ShapeDtypeStruct((g, m, n), out_dtype),
        compiler_params=_cparams(("parallel", "parallel", "arbitrary")),
        name="bmm",
    )(*((x, w) if res is None else (x, w, res)))


def mm(x, w, out_dtype=F32, res=None):
    lead = x.shape[:-1]
    n = w.shape[1]
    out = bmm(x.reshape((1, -1, x.shape[-1])), w[None], out_dtype,
              None if res is None else res.reshape((1, -1, n)))
    return out.reshape(lead + (n,))


def _hmm_kernel(x_ref, w_ref, o_ref):
    o_ref[...] = jnp.dot(x_ref[...].astype(BF16), w_ref[0], preferred_element_type=F32).astype(o_ref.dtype)


def hmm(x, w, out_dtype=F32):
    m = x.shape[0]
    n_heads, k, n = w.shape
    bm = _pick(m, _ROW_BLOCKS)
    return pl.pallas_call(
        _hmm_kernel,
        grid=(n_heads, m // bm),
        in_specs=[pl.BlockSpec((bm, k), lambda h, i: (i, h)),
                  pl.BlockSpec((1, k, n), lambda h, i: (h, 0, 0))],
        out_specs=pl.BlockSpec((bm, n), lambda h, i: (i, h)),
        out_shape=jax.ShapeDtypeStruct((m, n_heads * n), out_dtype),
        compiler_params=_cparams(("parallel", "parallel")),
        name="hmm",
    )(x, w)


def _ffn_in_kernel(*refs, t_len, bm, seq_in_block):
    if seq_in_block:
        x_ref, wg_ref, wu_ref, cw_ref, cb_ref, e_ref, hid_ref, tail_ref = refs
    else:
        x_ref, xp_ref, wg_ref, wu_ref, cw_ref, cb_ref, e_ref, hid_ref, tail_ref = refs
    x = x_ref[...]
    wg = wg_ref[...]
    u = jnp.dot(x, wg, preferred_element_type=F32)
    up = jnp.dot(x, wu_ref[...], preferred_element_type=F32)
    row = lax.broadcasted_iota(jnp.int32, u.shape, 0)
    if seq_in_block:
        t = row % t_len
        u1 = jnp.where(t < 1, e_ref[0], pltpu.roll(u, 1, axis=0))
        u2 = jnp.where(t < 2, e_ref[1], pltpu.roll(u, 2, axis=0))
        tail_ref[...] = u
    else:
        prev8 = jnp.dot(xp_ref[...], wg, preferred_element_type=F32)
        seq_start = (pl.program_id(0) * bm) % t_len == 0
        prev8 = jnp.where(seq_start, e_ref[0], prev8)
        u1 = jnp.where(row < 1, prev8[7:8], pltpu.roll(u, 1, axis=0))
        u2 = jnp.where(row < 1, prev8[6:7], jnp.where(row < 2, prev8[7:8], pltpu.roll(u, 2, axis=0)))
        tail_ref[0] = u[bm - 8:bm]
    conv = cb_ref[...] + cw_ref[0:1] * u2 + cw_ref[1:2] * u1 + cw_ref[2:3] * u
    hid_ref[...] = (conv * jax.nn.sigmoid(conv) * up).astype(hid_ref.dtype)


def ffn_in(h, prev, w_gate, w_up, conv_w, conv_b):
    bsz, t_len, d = h.shape
    f = w_gate.shape[1]
    m = bsz * t_len
    x = h.reshape(m, d)
    bn = _pick(f, (512, 256, 128))
    seq_in_block = t_len < ROW_BLOCK
    if seq_in_block:
        bm = _pick(m, tuple(c for c in _ROW_BLOCKS if c <= ROW_BLOCK and c % t_len == 0))
        tpos = jnp.arange(t_len)[None, :, None]
        e = jnp.stack([jnp.broadcast_to(prev[:, 1:2], (bsz, t_len, f)),
                       jnp.where(tpos == 0, prev[:, 0:1], prev[:, 1:2])]).reshape(2, m, f)
        lead = []
        lead_specs = []
        e_spec = pl.BlockSpec((2, bm, bn), lambda i, j: (0, i, j))
        tail_shape = jax.ShapeDtypeStruct((m, f), F32)
        tail_spec = pl.BlockSpec((bm, bn), lambda i, j: (i, j))
    else:
        bm = _pick(t_len, tuple(c for c in _ROW_BLOCKS if c <= ROW_BLOCK))
        e = jnp.pad(prev, ((0, 0), (6, 0), (0, 0)))
        lead = [x]
        lead_specs = [pl.BlockSpec((8, d), lambda i, j: (jnp.maximum(i * (bm // 8) - 1, 0), 0))]
        e_spec = pl.BlockSpec((1, 8, bn), lambda i, j: ((i * bm) // t_len, 0, j))
        tail_shape = jax.ShapeDtypeStruct((m // bm, 8, f), F32)
        tail_spec = pl.BlockSpec((1, 8, bn), lambda i, j: (i, 0, j))
    w_spec = pl.BlockSpec((d, bn), lambda i, j: (0, j))
    hid, tail = pl.pallas_call(
        functools.partial(_ffn_in_kernel, t_len=t_len, bm=bm, seq_in_block=seq_in_block),
        grid=(m // bm, f // bn),
        in_specs=([pl.BlockSpec((bm, d), lambda i, j: (i, 0))] + lead_specs
                  + [w_spec, w_spec, pl.BlockSpec((CONV_W, bn), lambda i, j: (0, j)),
                     pl.BlockSpec((1, bn), lambda i, j: (0, j)), e_spec]),
        out_specs=[pl.BlockSpec((bm, bn), lambda i, j: (i, j)), tail_spec],
        out_shape=[jax.ShapeDtypeStruct((m, f), BF16), tail_shape],
        compiler_params=_cparams(("parallel", "arbitrary")),
        name="ffn_in",
    )(x, *lead, w_gate, w_up, conv_w, conv_b.reshape(1, f), e)
    if seq_in_block:
        state = tail.reshape(bsz, t_len, f)[:, t_len - 2:]
    else:
        state = tail.reshape(bsz, t_len // bm, 8, f)[:, -1, 6:]
    return hid, state


def _dot(a, b, dims=((1,), (0,))):
    return lax.dot_general(a.astype(BF16), b.astype(BF16), (dims, ((), ())), preferred_element_type=F32)


_NT = ((1,), (1,))
_TN = ((0,), (0,))


def _wkv_chunk_kernel(r_ref, ld_ref, k_ref, v_ref, kk_ref, as_ref, s0_ref, sall_ref, y_ref, sout_ref, s_sc, *,
                      c, hb, n, ng, nb):
    del sall_ref
    w = hb * n
    rb = hb * c
    rows = nb * rb
    first = pl.program_id(2) == 0
    last = pl.program_id(2) == pl.num_programs(2) - 1
    ri = lax.broadcasted_iota(jnp.int32, (rb, w), 0)
    ci = lax.broadcasted_iota(jnp.int32, (rb, w), 1)
    head_mask = (ri // c) == (ci // n)
    bd = lambda z: jnp.where(head_mask, jnp.concatenate([z] * hb, axis=0), 0.0)
    trow = lax.broadcasted_iota(jnp.int32, (c, w), 0)
    tr = lax.broadcasted_iota(jnp.int32, (rows, rows), 0)
    tc_ = lax.broadcasted_iota(jnp.int32, (rows, rows), 1)
    same = (tr // c) == (tc_ // c)
    strict = same & ((tr % c) > (tc_ % c))
    incl = same & ((tr % c) >= (tc_ % c))
    eye = jnp.where(tr == tc_, 1.0, 0.0)
    grp = range(ng)
    bat = range(nb)
    lanes = [slice(gi * w, (gi + 1) * w) for gi in grp]
    cat = lambda parts: parts[0] if len(parts) == 1 else jnp.concatenate(parts, axis=0)
    rsl = lambda z, bi: z[bi * rb:(bi + 1) * rb]

    @pl.when(first)
    def _():
        sr = lax.broadcasted_iota(jnp.int32, (w, w), 0)
        sc_ = lax.broadcasted_iota(jnp.int32, (w, w), 1)
        for gi in grp:
            for bi in bat:
                s0 = jnp.concatenate([s0_ref[bi, gi * hb + h] for h in range(hb)], axis=0)
                s_sc[gi * nb + bi] = jnp.where((sr // n) == (sc_ // n),
                                               jnp.concatenate([s0] * hb, axis=1), 0.0)

    def prep(gi):
        parts = []
        for bi in bat:
            r, ld, k, v, kk, asig = (ref[bi, :, lanes[gi]]
                                     for ref in (r_ref, ld_ref, k_ref, v_ref, kk_ref, as_ref))
            lam = ld
            shift = 1
            while shift < c:
                lam = lam + jnp.where(trow >= shift, pltpu.roll(lam, shift, axis=0), 0.0)
                shift *= 2
            g_inv = jnp.exp(-lam)
            parts.append((bd(-kk * jnp.exp(lam - ld)), bd(r * jnp.exp(lam)), bd(kk * asig * g_inv),
                          bd(k * g_inv), bd(v), jnp.exp(lam[c - 1:c])))
        return tuple(cat([pt[i] for pt in parts]) for i in range(5)) + ([pt[5] for pt in parts],)

    a_bd, r_bd, b_bd, k_bd, v_bd, g_end = zip(*[prep(gi) for gi in grp])
    p = [_dot(jnp.concatenate([a_bd[gi], r_bd[gi]], axis=0),
              jnp.concatenate([b_bd[gi], k_bd[gi]], axis=0), _NT) for gi in grp]
    l_pow = [jnp.where(strict, p[gi][:rows, :rows], 0.0) for gi in grp]
    t_inv = [eye + l_pow[gi] for gi in grp]
    for _ in range(c.bit_length() - 2):
        l_pow = [_dot(l_pow[gi], l_pow[gi]) for gi in grp]
        t_inv = [t_inv[gi] + _dot(t_inv[gi], l_pow[gi]) for gi in grp]
    s = [[s_sc[gi * nb + bi] for bi in bat] for gi in grp]
    xs = [[_dot(jnp.concatenate([rsl(a_bd[gi], bi), rsl(r_bd[gi], bi)], axis=0), s[gi][bi], _NT)
           for bi in bat] for gi in grp]
    xs_a = [cat([xs[gi][bi][:rb] for bi in bat]) for gi in grp]
    xs_r = [cat([xs[gi][bi][rb:] for bi in bat]) for gi in grp]
    lkv = [_dot(jnp.where(strict, p[gi][:rows, rows:], 0.0), v_bd[gi]) for gi in grp]
    pkv = [_dot(jnp.where(incl, p[gi][rows:, rows:], 0.0), v_bd[gi]) for gi in grp]
    u = [_dot(t_inv[gi], xs_a[gi] + lkv[gi]) for gi in grp]
    y_bd = [xs_r[gi] + pkv[gi] + _dot(jnp.where(incl, p[gi][rows:, :rows], 0.0), u[gi]) for gi in grp]
    s_new = [[s[gi][bi] * g_end[gi][bi]
              + _dot(jnp.concatenate([rsl(u[gi], bi), rsl(v_bd[gi], bi)], axis=0),
                     jnp.concatenate([rsl(b_bd[gi], bi), rsl(k_bd[gi], bi)], axis=0) * g_end[gi][bi], _TN)
              for bi in bat] for gi in grp]
    for gi in grp:
        for bi in bat:
            y = y_bd[gi][bi * rb:bi * rb + c]
            for h in range(1, hb):
                y = y + y_bd[gi][bi * rb + h * c:bi * rb + (h + 1) * c]
            y_ref[bi, :, lanes[gi]] = y
            s_sc[gi * nb + bi] = s_new[gi][bi]

    @pl.when(last)
    def _():
        for gi in grp:
            for bi in bat:
                for h in range(hb):
                    sout_ref[bi, gi * hb + h] = s_new[gi][bi][h * n:(h + 1) * n, h * n:(h + 1) * n]


def wkv_chunked(r, ld, k, v, kk, asig, s0, n_heads, s_all, layer):
    bsz, t, d = r.shape
    n = d // n_heads
    hb = _pick(n_heads, (4, 2, 1))
    ng = _pick(n_heads // hb, (4, 2, 1))
    c = _pick(t, (WKV_CHUNK, 32, 16, 8))
    nb = _pick(bsz, tuple(x for x in (8, 4, 2, 1) if x * hb * c <= MXU_WIDTH))
    seq_spec = pl.BlockSpec((nb, c, ng * hb * n), lambda b, q, ch: (b, ch, q))
    st_spec = pl.BlockSpec((nb, ng * hb, n, n), lambda b, q, ch: (b, q, 0, 0))
    st_out_spec = pl.BlockSpec((None, nb, ng * hb, n, n), lambda b, q, ch: (layer, b, q, 0, 0))
    return pl.pallas_call(
        functools.partial(_wkv_chunk_kernel, c=c, hb=hb, n=n, ng=ng, nb=nb),
        grid=(bsz // nb, n_heads // (hb * ng), t // c),
        in_specs=[seq_spec] * 6 + [st_spec, pl.BlockSpec(memory_space=pl.ANY)],
        out_specs=[seq_spec, st_out_spec],
        out_shape=[jax.ShapeDtypeStruct((bsz, t, d), F32),
                   jax.ShapeDtypeStruct(s_all.shape, F32)],
        scratch_shapes=[pltpu.VMEM((ng * nb, hb * n, hb * n), F32)],
        input_output_aliases={7: 1},
        compiler_params=_cparams(("parallel", "parallel", "arbitrary")),
        name="wkv_chunked",
    )(r, ld, k, v, kk, asig, s0, s_all)


def _head_sum(x, n):
    wd = min(MXU_WIDTH, x.shape[1])
    gi = lax.broadcasted_iota(jnp.int32, (wd, wd), 0) // n
    gj = lax.broadcasted_iota(jnp.int32, (wd, wd), 1) // n
    g = jnp.where(gi == gj, 1.0, 0.0).astype(BF16)
    hi = x.astype(BF16)
    lo = (x - hi.astype(F32)).astype(BF16)
    parts = []
    for s in range(0, x.shape[1], wd):
        parts.append(jnp.dot(hi[:, s:s + wd], g, preferred_element_type=F32)
                     + jnp.dot(lo[:, s:s + wd], g, preferred_element_type=F32))
    return parts[0] if len(parts) == 1 else jnp.concatenate(parts, axis=1)


def _softplus(z):
    return jnp.maximum(z, 0.0) + jnp.log(1.0 + jnp.exp(-jnp.abs(z)))


def _rwkv_pre_kernel(*refs, n, has_v_lora):
    if has_v_lora:
        (k_ref, v_ref, hw_ref, ha_ref, w2_ref, a2_ref, w0_ref, a0_ref, kk_ref, ka_ref,
         vf_ref, hv_ref, v2_ref, v0_ref, ld_out, a_out, kk_out, k_out, v_out) = refs
    else:
        (k_ref, v_ref, hw_ref, ha_ref, w2_ref, a2_ref, w0_ref, a0_ref, kk_ref, ka_ref,
         ld_out, a_out, kk_out, k_out) = refs
    k = k_ref[...]
    w_pre = w0_ref[...] + jnp.dot(jnp.tanh(hw_ref[...]).astype(BF16), w2_ref[...], preferred_element_type=F32)
    ld_out[...] = -jnp.exp(-_softplus(-w_pre) - 0.5)
    a = jax.nn.sigmoid(a0_ref[...] + jnp.dot(ha_ref[...].astype(BF16), a2_ref[...], preferred_element_type=F32))
    a_out[...] = a
    kk = k * kk_ref[...]
    kk_out[...] = kk / jnp.maximum(jnp.sqrt(_head_sum(kk * kk, n)), 1e-12)
    k_out[...] = k * (1.0 + (a - 1.0) * ka_ref[...])
    if has_v_lora:
        v = v_ref[...]
        gate = jax.nn.sigmoid(v0_ref[...] + jnp.dot(hv_ref[...].astype(BF16), v2_ref[...],
                                                    preferred_element_type=F32))
        v_out[...] = v + (vf_ref[...] - v) * gate


def rwkv_pre(k, v, hw, ha, p, v_first, hv, n):
    m, d = k.shape
    bm = _pick(m, (256, 128, 64, 32, 16, 8))
    has_v_lora = p["v_lora"] is not None
    row = lambda width: pl.BlockSpec((bm, width), lambda i: (i, 0))
    full = lambda a: pl.BlockSpec(a.shape, lambda i: (0, 0))
    vec = lambda z: z.reshape(1, d)
    args = [k, v, hw, ha, p["w2"], p["a2"], vec(p["w0"]), vec(p["a0"]), vec(p["k_k"]), vec(p["k_a"])]
    specs = [row(d), row(d), row(hw.shape[1]), row(ha.shape[1])] + [full(a) for a in args[4:]]
    n_out = 4
    if has_v_lora:
        v0, _, v2 = p["v_lora"]
        extra = [v_first, hv, v2, vec(v0)]
        args += extra
        specs += [row(d), row(hv.shape[1]), full(v2), full(extra[3])]
        n_out = 5
    outs = pl.pallas_call(
        functools.partial(_rwkv_pre_kernel, n=n, has_v_lora=has_v_lora),
        grid=(m // bm,),
        in_specs=specs,
        out_specs=[row(d)] * n_out,
        out_shape=[jax.ShapeDtypeStruct((m, d), F32)] * n_out,
        compiler_params=_cparams(("parallel",)),
        name="rwkv_pre",
    )(*args)
    return tuple(outs) + ((v,) if not has_v_lora else ())


def _rwkv_post_kernel(y_ref, r_ref, k_ref, v_ref, hg_ref, g2_ref, lnw_ref, lnb_ref, rk_ref, o_ref, *, n):
    y = y_ref[...]
    mu = _head_sum(y, n) / n
    dlt = y - mu
    var = _head_sum(dlt * dlt, n) / n
    yn = dlt * lax.rsqrt(var + n * GN_EPS_PER_CHANNEL) * lnw_ref[...] + lnb_ref[...]
    bonus = _head_sum(r_ref[...] * k_ref[...] * rk_ref[...], n) * v_ref[...]
    g = jnp.dot(jax.nn.sigmoid(hg_ref[...]).astype(BF16), g2_ref[...], preferred_element_type=F32)
    o_ref[...] = ((yn + bonus) * g).astype(o_ref.dtype)


def rwkv_post(y, r, k, v, hg, p, n):
    m, d = y.shape
    bm = _pick(m, (256, 128, 64, 32, 16, 8))
    row = lambda width: pl.BlockSpec((bm, width), lambda i: (i, 0))
    full = lambda a: pl.BlockSpec(a.shape, lambda i: (0, 0))
    vec = lambda z: z.reshape(1, d)
    consts = [p["g2"], vec(p["ln_w"]), vec(p["ln_b"]), vec(p["r_k"])]
    return pl.pallas_call(
        functools.partial(_rwkv_post_kernel, n=n),
        grid=(m // bm,),
        in_specs=[row(d)] * 4 + [row(hg.shape[1])] + [full(a) for a in consts],
        out_specs=row(d),
        out_shape=jax.ShapeDtypeStruct((m, d), BF16),
        compiler_params=_cparams(("parallel",)),
        name="rwkv_post",
    )(y, r, k, v, hg, *consts)


def _flash_kernel(qn_ref, qr_ref, kn_ref, kr_ref, v_ref, o_ref, *, bq, n_q, hp, dh, dv, scale):
    qi = pl.program_id(2)
    dn = (((1,), (1,)), ((), ()))
    heads = range(hp)
    hs = [slice(h * dh, (h + 1) * dh) for h in heads]
    vs = [slice(h * dv, (h + 1) * dv) for h in heads]

    for i in range(n_q):
        @pl.when(qi == i)
        def _(i=i):
            kv_len = (i + 1) * bq
            past = kv_len - bq
            tri = (lax.broadcasted_iota(jnp.int32, (bq, bq), 1) <= lax.broadcasted_iota(jnp.int32, (bq, bq), 0))
            s = [(lax.dot_general(qn_ref[0, :, hs[h]], kn_ref[0, :kv_len, hs[h]], dn, preferred_element_type=F32)
                  + lax.dot_general(qr_ref[0, :, hs[h]], kr_ref[0, :kv_len, hs[h]], dn,
                                    preferred_element_type=F32)) * scale for h in heads]
            s_diag = [jnp.where(tri, s[h][:, past:], -jnp.inf) for h in heads]
            m = [jnp.max(s_diag[h], axis=-1, keepdims=True) for h in heads]
            if past:
                m = [jnp.maximum(m[h], jnp.max(s[h][:, :past], axis=-1, keepdims=True)) for h in heads]
            p_diag = [jnp.exp(s_diag[h] - m[h]) for h in heads]
            l = [jnp.sum(p_diag[h], axis=-1, keepdims=True) for h in heads]
            o = [jnp.dot(p_diag[h].astype(BF16), v_ref[0, past:kv_len, vs[h]], preferred_element_type=F32)
                 for h in heads]
            if past:
                p_past = [jnp.exp(s[h][:, :past] - m[h]) for h in heads]
                l = [l[h] + jnp.sum(p_past[h], axis=-1, keepdims=True) for h in heads]
                o = [o[h] + jnp.dot(p_past[h].astype(BF16), v_ref[0, :past, vs[h]], preferred_element_type=F32)
                     for h in heads]
            for h in heads:
                o_ref[0, :, vs[h]] = (o[h] / l[h]).astype(o_ref.dtype)


def flash_prompt(qn, qr, kn, kr, v, n_heads, scale):
    bsz, s_len, _ = qn.shape
    dh = qn.shape[-1] // n_heads
    dv = v.shape[-1] // n_heads
    bq = _pick(s_len, (512, 256, 128, 64, 32, 16, 8))
    n_q = s_len // bq
    hp = _pick(n_heads, (2, 1))
    q_spec = pl.BlockSpec((1, bq, hp * dh), lambda b, h, i: (b, i, h))
    k_spec = pl.BlockSpec((1, s_len, hp * dh), lambda b, h, i: (b, 0, h))
    return pl.pallas_call(
        functools.partial(_flash_kernel, bq=bq, n_q=n_q, hp=hp, dh=dh, dv=dv, scale=scale),
        grid=(bsz, n_heads // hp, n_q),
        in_specs=[q_spec, q_spec, k_spec, k_spec,
                  pl.BlockSpec((1, s_len, hp * dv), lambda b, h, i: (b, 0, h))],
        out_specs=pl.BlockSpec((1, bq, hp * dv), lambda b, h, i: (b, i, h)),
        out_shape=jax.ShapeDtypeStruct((bsz, s_len, n_heads * dv), BF16),
        compiler_params=_cparams(("parallel", "parallel", "arbitrary")),
        name="flash_prompt",
    )(qn, qr, kn, kr, v)


def _mla_head_kernel(*refs, n_heads, d_qk, half, is_query):
    tile = lambda z: jnp.concatenate([z] * n_heads, axis=1)
    if is_query:
        xn_ref, xr_ref, cos_ref, sin_a_ref, sin_b_ref, gn_ref, gr_ref, on_ref, or_ref = refs
        xr = xr_ref[...]
        width = xr.shape[1]
        xr = (xr * tile(cos_ref[...]) + pltpu.roll(xr, width - half, axis=1) * tile(sin_a_ref[...])
              + pltpu.roll(xr, half, axis=1) * tile(sin_b_ref[...]))
    else:
        xn_ref, kpe_ref, gn_ref, gr_ref, on_ref, or_ref = refs
        kpe = kpe_ref[...]
        pad = jnp.zeros((kpe.shape[0], xn_ref.shape[1] // n_heads - kpe.shape[1]), F32)
        xr = tile(jnp.concatenate([kpe, pad], axis=1))
    xn = xn_ref[...]
    inv = lax.rsqrt(_head_sum(xn * xn + xr * xr, xn.shape[1] // n_heads) / d_qk + RMS_EPS)
    on_ref[...] = (xn * inv * gn_ref[...]).astype(on_ref.dtype)
    or_ref[...] = (xr * inv * gr_ref[...]).astype(or_ref.dtype)


def mla_head_norm(x, gain_n, gain_r, n_heads, d_qk, d_rope, rope_tabs=None, k_pe=None):
    m = x.shape[0]
    is_query = rope_tabs is not None
    wd = gain_n.shape[1]
    dh = wd // n_heads
    bm = _pick(m, (256, 128, 64, 32, 16, 8))
    row = lambda width, blk=0: pl.BlockSpec((bm, width), lambda i, blk=blk: (i, blk))
    full = lambda a: pl.BlockSpec(a.shape, lambda i: (0, 0))
    if is_query:
        args = [x, x, *rope_tabs, gain_n, gain_r]
        specs = [row(wd, 0), row(wd, 1), row(dh), row(dh), row(dh), full(gain_n), full(gain_r)]
    else:
        args = [x, k_pe, gain_n, gain_r]
        specs = [row(wd), row(k_pe.shape[1]), full(gain_n), full(gain_r)]
    return pl.pallas_call(
        functools.partial(_mla_head_kernel, n_heads=n_heads, d_qk=d_qk, half=d_rope // 2, is_query=is_query),
        grid=(m // bm,),
        in_specs=specs,
        out_specs=[row(wd), row(wd)],
        out_shape=[jax.ShapeDtypeStruct((m, wd), BF16)] * 2,
        compiler_params=_cparams(("parallel",)),
        name="mla_head_norm",
    )(*args)


def _kscale_kernel(pt_ref, *refs, npg, n_heads, d_qk):
    del pt_ref
    c_refs = refs[:npg]
    kpe_refs = refs[npg:2 * npg]
    wt_ref = refs[2 * npg]
    s_ref, c_out, kpe_out = refs[2 * npg + 1:]
    page = c_refs[0].shape[1]
    pair = 2 if npg % 2 == 0 else 1
    for p in range(0, npg, pair):
        keys = slice(p * page, (p + pair) * page)
        c = jnp.concatenate([c_refs[p + q][0] for q in range(pair)], axis=0).astype(BF16)
        kpe_t = jnp.concatenate([kpe_refs[p + q][0] for q in range(pair)], axis=1)
        k_t = lax.dot_general(wt_ref[...], c, (((1,), (1,)), ((), ())), preferred_element_type=F32)
        sq = k_t * k_t
        ss = jnp.sum(sq.reshape(n_heads, sq.shape[0] // n_heads, sq.shape[1]), axis=1)
        pe = jnp.sum(kpe_t * kpe_t, axis=0, keepdims=True)
        s_ref[0, :, keys] = lax.rsqrt((ss + pe) / d_qk + RMS_EPS)
        c_out[0, keys, :] = c
        kpe_out[0, :, keys] = kpe_t.astype(BF16)


def paged_key_scale(cache_c, cache_kpe_t, page_table_flat, w_uk_t, n_heads, d_qk, npg):
    n_used = page_table_flat.shape[0]
    page, d_c = cache_c.shape[1:]
    d_r = cache_kpe_t.shape[1]
    n_grp = n_used // npg
    nk = npg * page

    def page_spec(shape, p):
        return pl.BlockSpec((1,) + shape, lambda g, pt, p=p: (pt[g * npg + p], 0, 0))

    in_specs = ([page_spec((page, d_c), p) for p in range(npg)] + [page_spec((d_r, page), p) for p in range(npg)]
                + [pl.BlockSpec(w_uk_t.shape, lambda g, pt: (0, 0))])
    out = lambda shape: pl.BlockSpec((1,) + shape, lambda g, pt: (g, 0, 0))
    grid_spec = pltpu.PrefetchScalarGridSpec(
        num_scalar_prefetch=1, grid=(n_grp,), in_specs=in_specs,
        out_specs=[out((n_heads, nk)), out((nk, d_c)), out((d_r, nk))])
    return pl.pallas_call(
        functools.partial(_kscale_kernel, npg=npg, n_heads=n_heads, d_qk=d_qk),
        grid_spec=grid_spec,
        out_shape=[jax.ShapeDtypeStruct((n_grp, n_heads, nk), F32),
                   jax.ShapeDtypeStruct((n_grp, nk, d_c), BF16),
                   jax.ShapeDtypeStruct((n_grp, d_r, nk), BF16)],
        compiler_params=_cparams(("arbitrary",)),
        name="paged_key_scale",
    )(page_table_flat, *([cache_c] * npg), *([cache_kpe_t] * npg), w_uk_t)


def _paged_attn_kernel(ql_ref, qr_ref, cnew_ref, pnew_ref, snew_ref, c_ref, p_ref, s_ref, o_ref,
                       m_sc, l_sc, acc_sc, *, bb, n_heads, t_len, scale):
    g = pl.program_id(1)
    rows = n_heads * t_len
    dn = (((1,), (1,)), ((), ()))

    def process(c, kpe_t, key_scale, mask):
        bs = range(bb)
        sc = [lax.dot_general(ql_ref[b], c[b], dn, preferred_element_type=F32)
              + jnp.dot(qr_ref[b], kpe_t[b], preferred_element_type=F32) for b in bs]
        sc = [sc[b] * (jnp.concatenate([key_scale[b]] * t_len, axis=0) * scale) for b in bs]
        if mask is not None:
            sc = [jnp.where(mask, sc[b], -jnp.inf) for b in bs]
        m_new = [jnp.maximum(m_sc[b], jnp.max(sc[b], axis=-1, keepdims=True)) for b in bs]
        alpha = [jnp.exp(m_sc[b] - m_new[b]) for b in bs]
        p = [jnp.exp(sc[b] - m_new[b]) for b in bs]
        pv = [jnp.dot(p[b].astype(BF16), c[b], preferred_element_type=F32) for b in bs]
        for b in bs:
            l_sc[b] = l_sc[b] * alpha[b] + jnp.sum(p[b], axis=-1, keepdims=True)
            acc_sc[b] = acc_sc[b] * alpha[b] + pv[b]
            m_sc[b] = m_new[b]

    @pl.when(g == 0)
    def _():
        m_sc[...] = jnp.full(m_sc.shape, -jnp.inf, F32)
        l_sc[...] = jnp.zeros(l_sc.shape, F32)
        acc_sc[...] = jnp.zeros(acc_sc.shape, F32)
        nk = cnew_ref.shape[1]
        t_row = lax.broadcasted_iota(jnp.int32, (rows, nk), 0) // n_heads
        key = lax.broadcasted_iota(jnp.int32, (rows, nk), 1)
        process([cnew_ref[b] for b in range(bb)], [pnew_ref[b] for b in range(bb)],
                [snew_ref[b] for b in range(bb)], key <= t_row)

    process([c_ref[b, 0] for b in range(bb)], [p_ref[b, 0] for b in range(bb)],
            [s_ref[b, 0] for b in range(bb)], None)

    @pl.when(g == pl.num_programs(1) - 1)
    def _():
        for b in range(bb):
            o_ref[b] = acc_sc[b] / l_sc[b]


def paged_attention(q_lat, q_r, c_new, kpe_new_t, s_new, c_past, kpe_past_t, s_past, n_heads, t_len, scale):
    bsz, rows, d_c = q_lat.shape
    d_r = q_r.shape[2]
    _, n_grp, nk, _ = c_past.shape
    kn = c_new.shape[1]
    bb = _pick(bsz, (4, 2, 1))

    def per_b(shape):
        return pl.BlockSpec((bb,) + shape, lambda b, g: (b, 0, 0))

    def per_g(shape):
        return pl.BlockSpec((bb, 1) + shape, lambda b, g: (b, g, 0, 0))

    return pl.pallas_call(
        functools.partial(_paged_attn_kernel, bb=bb, n_heads=n_heads, t_len=t_len, scale=scale),
        grid=(bsz // bb, n_grp),
        in_specs=[per_b((rows, d_c)), per_b((rows, d_r)), per_b((kn, d_c)), per_b((d_r, kn)),
                  per_b((n_heads, kn)), per_g((nk, d_c)), per_g((d_r, nk)), per_g((n_heads, nk))],
        out_specs=per_b((rows, d_c)),
        out_shape=jax.ShapeDtypeStruct((bsz, rows, d_c), F32),
        scratch_shapes=[pltpu.VMEM((bb, rows, 1), F32), pltpu.VMEM((bb, rows, 1), F32),
                        pltpu.VMEM((bb, rows, d_c), F32)],
        compiler_params=_cparams(("parallel", "arbitrary")),
        name="paged_attention",
    )(q_lat, q_r, c_new, kpe_new_t, s_new, c_past, kpe_past_t, s_past)


def _rms_norm(x, g):
    return x * lax.rsqrt(jnp.mean(x * x, axis=-1, keepdims=True) + RMS_EPS) * g


def _rope(x, pos):
    half = x.shape[-1] // 2
    inv = ROPE_BASE ** (-jnp.arange(half, dtype=F32) / half)
    ang = pos.astype(F32)[:, None] * inv[None, :]
    cos = jnp.cos(ang)[None, :, None, :]
    sin = jnp.sin(ang)[None, :, None, :]
    x1, x2 = x[..., :half], x[..., half:]
    return jnp.concatenate([x1 * cos - x2 * sin, x1 * sin + x2 * cos], axis=-1)


def _conv_ffn(x, prev, p):
    h = _rms_norm(x, p["norm"]).astype(BF16)
    hid, state = ffn_in(h, prev, p["w_gate"], p["w_up"], p["conv_w"], p["conv_b"])
    return mm(hid, p["w_down"], res=x.reshape(hid.shape[0], -1)).reshape(x.shape), state


def _rwkv7_time_mix(x, shift_prev, wkv_prev, v_first, p, wkv_all, layer):
    bsz, t, d = x.shape
    n_heads, n = p["r_k"].shape
    h = _rms_norm(x, p["norm"])
    prev = jnp.concatenate([shift_prev[:, None, :], h[:, :-1]], axis=1)
    xx = prev - h
    xr, xw, xk, xv, xa, xg = ((h + xx * p["mix"][i]).astype(BF16) for i in range(6))
    flat = lambda z: z.reshape(bsz * t, -1)
    r = mm(flat(xr), p["wr"])
    k = mm(flat(xk), p["wk"])
    v = mm(flat(xv), p["wv"])
    hv = None if p["v_lora"] is None else mm(flat(xv), p["v_lora"][1])
    log_decay, a, kk, k, v = rwkv_pre(k, v, mm(flat(xw), p["w1"]), mm(flat(xa), p["a1"]), p, v_first, hv, n)
    if p["v_lora"] is None:
        v_first = v
    seq = lambda z: z.reshape(bsz, t, d)
    y, s = wkv_chunked(seq(r), seq(log_decay), seq(k), seq(v), seq(kk), seq(a), wkv_prev, n_heads,
                       wkv_all, layer)
    out = rwkv_post(flat(y), r, k, v, mm(flat(xg), p["g1"]), p, n)
    return mm(out, p["wo"], res=flat(x)).reshape(x.shape), h[:, -1], s, v_first


def _mla_kv_side(x, pos, kv):
    u = _rms_norm(x, kv["norm"]).astype(BF16)
    d_c = kv["norm_c"].shape[0]
    ckr = mm(u, kv["w_dkv_kr"])
    c = _rms_norm(ckr[..., :d_c], kv["norm_c"])
    k_pe = _rope(ckr[..., d_c:][:, :, None, :], pos)[:, :, 0, :]
    return c, k_pe


def _head_gains(g, n_heads, d_nope, other=None):
    if other is not None:
        g = g * other
    g_r = jnp.pad(g[d_nope:], (0, 2 * d_nope - g.shape[0]))
    return jnp.tile(g[:d_nope], n_heads)[None], jnp.tile(g_r, n_heads)[None]


def _mla_query(x, pos, p, kv, n_heads, d_nope, d_rope, absorb_key_gain):
    bsz, t, _ = x.shape
    h = _rms_norm(x, p["norm"]).astype(BF16)
    cq = _rms_norm(mm(h, p["w_dq"]), p["norm_cq"]).astype(BF16)
    q = mm(cq.reshape(bsz * t, -1), p["w_uq"])
    half = d_rope // 2
    inv = ROPE_BASE ** (-jnp.arange(half, dtype=F32) / half)
    ang = pos.astype(F32)[:, None] * inv[None, :]
    cos, sin, zero = jnp.cos(ang), jnp.sin(ang), jnp.zeros((t, half), F32)
    lane_pad = jnp.zeros((t, d_nope - d_rope), F32)
    tabs = tuple(jnp.tile(jnp.concatenate(parts + [lane_pad], axis=1), (bsz, 1))
                 for parts in ([cos, cos], [-sin, zero], [zero, sin]))
    g_n, g_r = _head_gains(p["norm_q"], n_heads, d_nope, kv["norm_k"] if absorb_key_gain else None)
    return mla_head_norm(q, g_n, g_r, n_heads, d_nope + d_rope, d_rope, rope_tabs=tabs)


def _prompt_attend_fn(c, k_pe, kv, n_heads, d_nope):
    bsz, s_len, _ = c.shape
    d_rope = k_pe.shape[-1]
    d_qk = d_nope + d_rope
    g_n, g_r = _head_gains(kv["norm_k"], n_heads, d_nope)
    k_n, k_r = mla_head_norm(mm(c.reshape(bsz * s_len, -1), kv["w_uk_2d"]), g_n, g_r, n_heads, d_qk, d_rope,
                             k_pe=k_pe.reshape(bsz * s_len, d_rope))
    v = mm(c, kv["w_uv_2d"], BF16)
    seq = lambda z: z.reshape(bsz, s_len, -1)

    def attend(q):
        return flash_prompt(seq(q[0]), seq(q[1]), seq(k_n), seq(k_r), v, n_heads, d_qk ** -0.5)

    return attend


def _sample_attend_fn(c_new, kpe_new, cache_c, cache_kpe, page_table, kv, n_heads, d_nope):
    bsz, t, d_c = c_new.shape
    d_r = kpe_new.shape[-1]
    d_qk = d_nope + d_r
    page = cache_c.shape[1]
    pt_flat = page_table.reshape(-1)
    npg = _pick(page_table.shape[1], (PAGES_PER_STEP, 8, 4, 2, 1))

    k_nope_new = mm(c_new, kv["w_uk_2d"]).reshape(bsz, t, n_heads, d_nope)
    ss = jnp.sum(k_nope_new * k_nope_new, axis=-1) + jnp.sum(kpe_new * kpe_new, axis=-1)[..., None]
    s_new = lax.rsqrt(ss / d_qk + RMS_EPS)
    kn = page
    c_new_p = jnp.pad(c_new.astype(BF16), ((0, 0), (0, kn - t), (0, 0)))
    kpe_new_t = jnp.pad(kpe_new.astype(BF16).transpose(0, 2, 1), ((0, 0), (0, 0), (0, kn - t)))
    s_new_p = jnp.pad(s_new.transpose(0, 2, 1), ((0, 0), (0, 0), (0, kn - t)), constant_values=1.0)

    s_past, c_past, kpe_past_t = paged_key_scale(cache_c, cache_kpe.transpose(0, 2, 1), pt_flat, kv["w_uk_t"],
                                                 n_heads, d_qk, npg)
    n_grp = page_table.shape[1] // npg
    nk = npg * page
    s_past = s_past.reshape(bsz, n_grp, n_heads, nk)
    c_past = c_past.reshape(bsz, n_grp, nk, d_c)
    kpe_past_t = kpe_past_t.reshape(bsz, n_grp, d_r, nk)
    scale = d_qk ** -0.5

    def attend(q):
        q_lat = hmm(q[0], kv["w_uk_hdc"], BF16).reshape(bsz, t * n_heads, d_c)
        q_r = q[1].reshape(bsz, t * n_heads, d_nope)[..., :d_r]
        o_lat = paged_attention(q_lat, q_r, c_new_p, kpe_new_t, s_new_p, c_past, kpe_past_t, s_past,
                                n_heads, t, scale)
        o = hmm(o_lat.reshape(bsz * t, n_heads * d_c), kv["w_uv_hcd"])
        return o.reshape(bsz, t, -1)

    return attend


def kernel(x_prompt, x_sample, cache_kv_latent, cache_k_pe, state_wkv, state_shift, state_conv, page_table,
           ffn_norm, ffn_w_gate, ffn_w_up, ffn_conv_w, ffn_conv_b, ffn_w_down,
           rw_norm, rw_mix, rw_w0, rw_w1, rw_w2, rw_a0, rw_a1, rw_a2, rw_v0, rw_v1, rw_v2,
           rw_g1, rw_g2, rw_k_k, rw_k_a, rw_r_k, rw_wr, rw_wk, rw_wv, rw_ln_w, rw_ln_b, rw_wo,
           kv_norm, kv_w_dkv, kv_norm_c, kv_w_kr, kv_w_uk, kv_w_uv, kv_norm_k,
           mla_norm, mla_w_dq, mla_norm_cq, mla_w_uq, mla_norm_q, mla_wo):
    depth = ffn_norm.shape[0]
    n_a = rw_norm.shape[0]
    d_c, n_heads, d_nope = kv_w_uk.shape
    d_v = kv_w_uv.shape[2]
    bf = lambda z: z.astype(BF16)

    ffn = [dict(norm=ffn_norm[l], w_gate=bf(ffn_w_gate[l]), w_up=bf(ffn_w_up[l]), conv_w=ffn_conv_w[l],
                conv_b=ffn_conv_b[l], w_down=bf(ffn_w_down[l])) for l in range(depth)]
    rw = [dict(norm=rw_norm[l], mix=rw_mix[l], w0=rw_w0[l], w1=bf(rw_w1[l]), w2=bf(rw_w2[l]),
               a0=rw_a0[l], a1=bf(rw_a1[l]), a2=bf(rw_a2[l]), g1=bf(rw_g1[l]), g2=bf(rw_g2[l]),
               k_k=rw_k_k[l], k_a=rw_k_a[l], r_k=rw_r_k[l], wr=bf(rw_wr[l]), wk=bf(rw_wk[l]),
               wv=bf(rw_wv[l]), ln_w=rw_ln_w[l], ln_b=rw_ln_b[l], wo=bf(rw_wo[l]),
               v_lora=None if l == 0 else (rw_v0[l - 1], bf(rw_v1[l - 1]), bf(rw_v2[l - 1])))
          for l in range(n_a)]
    kv = dict(norm=kv_norm, norm_c=kv_norm_c, norm_k=kv_norm_k,
              w_dkv_kr=bf(jnp.concatenate([kv_w_dkv, kv_w_kr], axis=1)),
              w_uk_2d=bf(kv_w_uk.reshape(d_c, n_heads * d_nope)),
              w_uv_2d=bf(kv_w_uv.reshape(d_c, n_heads * d_v)),
              w_uk_t=bf(kv_w_uk.reshape(d_c, n_heads * d_nope).T),
              w_uk_hdc=bf(kv_w_uk.transpose(1, 2, 0)),
              w_uv_hcd=bf(kv_w_uv.transpose(1, 0, 2)))
    d_qc = mla_w_uq.shape[1]
    d_rope = kv_w_kr.shape[1]

    def split_heads(w_uq):
        w_r = jnp.pad(w_uq[:, :, d_nope:], ((0, 0), (0, 0), (0, d_nope - d_rope)))
        return jnp.concatenate([w_uq[:, :, :d_nope].reshape(d_qc, -1), w_r.reshape(d_qc, -1)], axis=1)

    mla = [dict(norm=mla_norm[b], w_dq=bf(mla_w_dq[b]), norm_cq=mla_norm_cq[b],
                w_uq=bf(split_heads(mla_w_uq[b])), norm_q=mla_norm_q[b],
                wo=bf(mla_wo[b])) for b in range(depth - n_a)]

    def trunk(x, pos, shift_in, wkv_in, conv_in, make_attend, absorb_key_gain):
        v_first = attend = c = k_pe = None
        shifts, convs = [], []
        wkvs = jnp.zeros(wkv_in.shape, F32)
        for layer in range(depth):
            if layer < n_a:
                x, sh, wkvs, v_first = _rwkv7_time_mix(x, shift_in[layer], wkv_in[layer], v_first, rw[layer],
                                                       wkvs, layer)
                shifts.append(sh)
            else:
                if layer == n_a:
                    c, k_pe = _mla_kv_side(x, pos, kv)
                    attend = make_attend(c, k_pe)
                p = mla[layer - n_a]
                q = _mla_query(x, pos, p, kv, n_heads, d_nope, d_rope, absorb_key_gain)
                x = mm(attend(q), p["wo"], res=x)
            x, cv = _conv_ffn(x, conv_in[layer], ffn[layer])
            convs.append(cv)
        return x, c, k_pe, wkvs, jnp.stack(shifts), jnp.stack(convs)

    bp, sp, d = x_prompt.shape
    n_rw_heads, n = rw_r_k.shape[1:]
    d_ff = ffn_w_gate.shape[2]
    out_p = trunk(
        x_prompt, jnp.arange(sp),
        jnp.zeros((n_a, bp, d), F32), jnp.zeros((n_a, bp, n_rw_heads, n, n), F32),
        jnp.zeros((depth, bp, CONV_W - 1, d_ff), F32),
        lambda c, kp: _prompt_attend_fn(c, kp, kv, n_heads, d_nope), False)

    past_len = page_table.shape[1] * cache_kv_latent.shape[1]
    out_s = trunk(
        x_sample, past_len + jnp.arange(x_sample.shape[1]), state_shift, state_wkv, state_conv,
        lambda c, kp: _sample_attend_fn(c, kp, cache_kv_latent, cache_k_pe, page_table, kv, n_heads, d_nope),
        True)

    return (out_p[0], out_s[0]) + tuple(out_p[1:]) + tuple(out_s[1:])
```

```python
import functools

import jax
import jax.numpy as jnp
from jax import lax
from jax.experimental import pallas as pl
from jax.experimental.pallas import tpu as pltpu

F32 = jnp.float32
BF16 = jnp.bfloat16

RMS_EPS = 1e-6
ROPE_BASE = 10000.0
GN_EPS_PER_CHANNEL = 1e-5
CONV_W = 3

VMEM_LIMIT_BYTES = 52 * 1024 * 1024
LANES = 128
MXU_WIDTH = 256
PAGES_PER_STEP = 32
ROW_BLOCK = 1024
WKV_CHUNK = 64
_ROW_BLOCKS = (1024, 512, 256, 128, 64, 32, 16, 8)


def _pick(n, candidates):
    for c in candidates:
        if c <= n and n % c == 0:
            return c
    return n


def _cparams(sem):
    return pltpu.CompilerParams(dimension_semantics=sem, vmem_limit_bytes=VMEM_LIMIT_BYTES)


def _mm_kernel(x_ref, w_ref, o_ref):
    o_ref[0] = jnp.dot(x_ref[0].astype(BF16), w_ref[0],
                       preferred_element_type=F32).astype(o_ref.dtype)


def _mm_res_kernel(x_ref, w_ref, r_ref, o_ref):
    o_ref[0] = r_ref[0] + jnp.dot(x_ref[0].astype(BF16), w_ref[0], preferred_element_type=F32)


def bmm(x, w, out_dtype=F32, res=None):
    g, m, k = x.shape
    n = w.shape[2]
    bm = _pick(m, (1024, 512, 256, 128, 64, 32, 16, 8))
    bn = n if n <= 1024 else _pick(n, (512, 256, 128))
    out_spec = pl.BlockSpec((1, bm, bn), lambda a, i, j: (a, i, j))
    in_specs = [pl.BlockSpec((1, bm, k), lambda a, i, j: (a, i, 0)),
                pl.BlockSpec((1, k, bn), lambda a, i, j: (a, 0, j))]
    return pl.pallas_call(
        _mm_kernel if res is None else _mm_res_kernel,
        grid=(g, m // bm, n // bn),
        in_specs=in_specs if res is None else in_specs + [out_spec],
        out_specs=out_spec,
        out_shape=jax.ShapeDtypeStruct((g, m, n), out_dtype),
        compiler_params=_cparams(("parallel", "parallel", "arbitrary")),
        name="bmm",
    )(*((x, w) if res is None else (x, w, res)))


def mm(x, w, out_dtype=F32, res=None):
    lead = x.shape[:-1]
    n = w.shape[1]
    out = bmm(x.reshape((1, -1, x.shape[-1])), w[None], out_dtype,
              None if res is None else res.reshape((1, -1, n)))
    return out.reshape(lead + (n,))


def _hmm_kernel(x_ref, w_ref, o_ref):
    o_ref[...] = jnp.dot(x_ref[...].astype(BF16), w_ref[0], preferred_element_type=F32).astype(o_ref.dtype)


def hmm(x, w, out_dtype=F32):
    m = x.shape[0]
    n_heads, k, n = w.shape
    bm = _pick(m, _ROW_BLOCKS)
    return pl.pallas_call(
        _hmm_kernel,
        grid=(n_heads, m // bm),
        in_specs=[pl.BlockSpec((bm, k), lambda h, i: (i, h)),
                  pl.BlockSpec((1, k, n), lambda h, i: (h, 0, 0))],
        out_specs=pl.BlockSpec((bm, n), lambda h, i: (i, h)),
        out_shape=jax.ShapeDtypeStruct((m, n_heads * n), out_dtype),
        compiler_params=_cparams(("parallel", "parallel")),
        name="hmm",
    )(x, w)


def _ffn_in_kernel(*refs, t_len, bm, seq_in_block):
    if seq_in_block:
        x_ref, g_ref, wg_ref, wu_ref, cw_ref, cb_ref, e_ref, hid_ref, tail_ref, h_sc = refs
    else:
        x_ref, xp_ref, g_ref, wg_ref, wu_ref, cw_ref, cb_ref, e_ref, hid_ref, tail_ref, h_sc = refs

    def norm(z):
        return (z * lax.rsqrt(jnp.mean(z * z, axis=-1, keepdims=True) + RMS_EPS) * g_ref[...]).astype(BF16)

    @pl.when(pl.program_id(1) == 0)
    def _():
        h_sc[...] = norm(x_ref[...])

    x = h_sc[...]
    wg = wg_ref[...]
    u = jnp.dot(x, wg, preferred_element_type=F32)
    up = jnp.dot(x, wu_ref[...], preferred_element_type=F32)
    row = lax.broadcasted_iota(jnp.int32, u.shape, 0)
    if seq_in_block:
        t = row % t_len
        u1 = jnp.where(t < 1, e_ref[0], pltpu.roll(u, 1, axis=0))
        u2 = jnp.where(t < 2, e_ref[1], pltpu.roll(u, 2, axis=0))
        tail_ref[...] = u
    else:
        prev8 = jnp.dot(norm(xp_ref[...]), wg, preferred_element_type=F32)
        seq_start = (pl.program_id(0) * bm) % t_len == 0
        prev8 = jnp.where(seq_start, e_ref[0], prev8)
        u1 = jnp.where(row < 1, prev8[7:8], pltpu.roll(u, 1, axis=0))
        u2 = jnp.where(row < 1, prev8[6:7], jnp.where(row < 2, prev8[7:8], pltpu.roll(u, 2, axis=0)))
        tail_ref[0] = u[bm - 8:bm]
    conv = cb_ref[...] + cw_ref[0:1] * u2 + cw_ref[1:2] * u1 + cw_ref[2:3] * u
    hid_ref[...] = (conv * jax.nn.sigmoid(conv) * up).astype(hid_ref.dtype)


def ffn_in(x, gain, prev, w_gate, w_up, conv_w, conv_b):
    bsz, t_len, d = x.shape
    f = w_gate.shape[1]
    m = bsz * t_len
    x = x.reshape(m, d)
    bn = _pick(f, (512, 256, 128))
    seq_in_block = t_len < ROW_BLOCK
    if seq_in_block:
        bm = _pick(m, tuple(c for c in _ROW_BLOCKS if c <= ROW_BLOCK and c % t_len == 0))
        tpos = jnp.arange(t_len)[None, :, None]
        e = jnp.stack([jnp.broadcast_to(prev[:, 1:2], (bsz, t_len, f)),
                       jnp.where(tpos == 0, prev[:, 0:1], prev[:, 1:2])]).reshape(2, m, f)
        lead = []
        lead_specs = []
        e_spec = pl.BlockSpec((2, bm, bn), lambda i, j: (0, i, j))
        tail_shape = jax.ShapeDtypeStruct((m, f), F32)
        tail_spec = pl.BlockSpec((bm, bn), lambda i, j: (i, j))
    else:
        bm = _pick(t_len, tuple(c for c in _ROW_BLOCKS if c <= ROW_BLOCK))
        e = jnp.pad(prev, ((0, 0), (6, 0), (0, 0)))
        lead = [x]
        lead_specs = [pl.BlockSpec((8, d), lambda i, j: (jnp.maximum(i * (bm // 8) - 1, 0), 0))]
        e_spec = pl.BlockSpec((1, 8, bn), lambda i, j: ((i * bm) // t_len, 0, j))
        tail_shape = jax.ShapeDtypeStruct((m // bm, 8, f), F32)
        tail_spec = pl.BlockSpec((1, 8, bn), lambda i, j: (i, 0, j))
    w_spec = pl.BlockSpec((d, bn), lambda i, j: (0, j))
    hid, tail = pl.pallas_call(
        functools.partial(_ffn_in_kernel, t_len=t_len, bm=bm, seq_in_block=seq_in_block),
        grid=(m // bm, f // bn),
        in_specs=([pl.BlockSpec((bm, d), lambda i, j: (i, 0))] + lead_specs
                  + [pl.BlockSpec((1, d), lambda i, j: (0, 0)),
                     w_spec, w_spec, pl.BlockSpec((CONV_W, bn), lambda i, j: (0, j)),
                     pl.BlockSpec((1, bn), lambda i, j: (0, j)), e_spec]),
        out_specs=[pl.BlockSpec((bm, bn), lambda i, j: (i, j)), tail_spec],
        out_shape=[jax.ShapeDtypeStruct((m, f), BF16), tail_shape],
        scratch_shapes=[pltpu.VMEM((bm, d), BF16)],
        compiler_params=_cparams(("parallel", "arbitrary")),
        name="ffn_in",
    )(x, *lead, gain.reshape(1, d), w_gate, w_up, conv_w, conv_b.reshape(1, f), e)
    if seq_in_block:
        state = tail.reshape(bsz, t_len, f)[:, t_len - 2:]
    else:
        state = tail.reshape(bsz, t_len // bm, 8, f)[:, -1, 6:]
    return hid, state


def _dot(a, b, dims=((1,), (0,))):
    return lax.dot_general(a.astype(BF16), b.astype(BF16), (dims, ((), ())), preferred_element_type=F32)


_NT = ((1,), (1,))
_TN = ((0,), (0,))


def _wkv_chunk_kernel(r_ref, ld_ref, k_ref, v_ref, kk_ref, as_ref, s0_ref, sall_ref, y_ref, sout_ref, s_sc, *,
                      c, hb, n, ng, nb):
    del sall_ref
    w = hb * n
    rb = hb * c
    rows = nb * rb
    first = pl.program_id(2) == 0
    last = pl.program_id(2) == pl.num_programs(2) - 1
    ri = lax.broadcasted_iota(jnp.int32, (rb, w), 0)
    ci = lax.broadcasted_iota(jnp.int32, (rb, w), 1)
    head_mask = (ri // c) == (ci // n)
    bd = lambda z: jnp.where(head_mask, jnp.concatenate([z] * hb, axis=0), 0.0)
    trow = lax.broadcasted_iota(jnp.int32, (c, w), 0)
    tr = lax.broadcasted_iota(jnp.int32, (rows, rows), 0)
    tc_ = lax.broadcasted_iota(jnp.int32, (rows, rows), 1)
    same = (tr // c) == (tc_ // c)
    strict = same & ((tr % c) > (tc_ % c))
    incl = same & ((tr % c) >= (tc_ % c))
    eye = jnp.where(tr == tc_, 1.0, 0.0)
    grp = range(ng)
    bat = range(nb)
    lanes = [slice(gi * w, (gi + 1) * w) for gi in grp]
    cat = lambda parts: parts[0] if len(parts) == 1 else jnp.concatenate(parts, axis=0)
    rsl = lambda z, bi: z[bi * rb:(bi + 1) * rb]

    @pl.when(first)
    def _():
        sr = lax.broadcasted_iota(jnp.int32, (w, w), 0)
        sc_ = lax.broadcasted_iota(jnp.int32, (w, w), 1)
        for gi in grp:
            for bi in bat:
                s0 = jnp.concatenate([s0_ref[bi, gi * hb + h] for h in range(hb)], axis=0)
                s_sc[gi * nb + bi] = jnp.where((sr // n) == (sc_ // n),
                                               jnp.concatenate([s0] * hb, axis=1), 0.0)

    def prep(gi):
        parts = []
        for bi in bat:
            r, ld, k, v, kk, asig = (ref[bi, :, lanes[gi]]
                                     for ref in (r_ref, ld_ref, k_ref, v_ref, kk_ref, as_ref))
            lam = ld
            shift = 1
            while shift < c:
                lam = lam + jnp.where(trow >= shift, pltpu.roll(lam, shift, axis=0), 0.0)
                shift *= 2
            g_inv = jnp.exp(-lam)
            parts.append((bd(-kk * jnp.exp(lam - ld)), bd(r * jnp.exp(lam)), bd(kk * asig * g_inv),
                          bd(k * g_inv), bd(v), jnp.exp(lam[c - 1:c])))
        return tuple(cat([pt[i] for pt in parts]) for i in range(5)) + ([pt[5] for pt in parts],)

    a_bd, r_bd, b_bd, k_bd, v_bd, g_end = zip(*[prep(gi) for gi in grp])
    p = [_dot(jnp.concatenate([a_bd[gi], r_bd[gi]], axis=0),
              jnp.concatenate([b_bd[gi], k_bd[gi]], axis=0), _NT) for gi in grp]
    l_pow = [jnp.where(strict, p[gi][:rows, :rows], 0.0) for gi in grp]
    t_inv = [eye + l_pow[gi] for gi in grp]
    for _ in range(c.bit_length() - 2):
        l_pow = [_dot(l_pow[gi], l_pow[gi]) for gi in grp]
        t_inv = [t_inv[gi] + _dot(t_inv[gi], l_pow[gi]) for gi in grp]
    s = [[s_sc[gi * nb + bi] for bi in bat] for gi in grp]
    xs = [[_dot(jnp.concatenate([rsl(a_bd[gi], bi), rsl(r_bd[gi], bi)], axis=0), s[gi][bi], _NT)
           for bi in bat] for gi in grp]
    xs_a = [cat([xs[gi][bi][:rb] for bi in bat]) for gi in grp]
    xs_r = [cat([xs[gi][bi][rb:] for bi in bat]) for gi in grp]
    lkv = [_dot(jnp.where(strict, p[gi][:rows, rows:], 0.0), v_bd[gi]) for gi in grp]
    pkv = [_dot(jnp.where(incl, p[gi][rows:, rows:], 0.0), v_bd[gi]) for gi in grp]
    u = [_dot(t_inv[gi], xs_a[gi] + lkv[gi]) for gi in grp]
    y_bd = [xs_r[gi] + pkv[gi] + _dot(jnp.where(incl, p[gi][rows:, :rows], 0.0), u[gi]) for gi in grp]
    s_new = [[s[gi][bi] * g_end[gi][bi]
              + _dot(jnp.concatenate([rsl(u[gi], bi), rsl(v_bd[gi], bi)], axis=0),
                     jnp.concatenate([rsl(b_bd[gi], bi), rsl(k_bd[gi], bi)], axis=0) * g_end[gi][bi], _TN)
              for bi in bat] for gi in grp]
    for gi in grp:
        for bi in bat:
            y = y_bd[gi][bi * rb:bi * rb + c]
            for h in range(1, hb):
                y = y + y_bd[gi][bi * rb + h * c:bi * rb + (h + 1) * c]
            y_ref[bi, :, lanes[gi]] = y
            s_sc[gi * nb + bi] = s_new[gi][bi]

    @pl.when(last)
    def _():
        for gi in grp:
            for bi in bat:
                for h in range(hb):
                    sout_ref[bi, gi * hb + h] = s_new[gi][bi][h * n:(h + 1) * n, h * n:(h + 1) * n]


def wkv_chunked(r, ld, k, v, kk, asig, s0, n_heads, s_all, layer):
    bsz, t, d = r.shape
    n = d // n_heads
    hb = _pick(n_heads, (4, 2, 1))
    ng = _pick(n_heads // hb, (4, 2, 1))
    c = _pick(t, (WKV_CHUNK, 32, 16, 8))
    nb = _pick(bsz, tuple(x for x in (8, 4, 2, 1) if x * hb * c <= MXU_WIDTH))
    seq_spec = pl.BlockSpec((nb, c, ng * hb * n), lambda b, q, ch: (b, ch, q))
    st_spec = pl.BlockSpec((nb, ng * hb, n, n), lambda b, q, ch: (b, q, 0, 0))
    st_out_spec = pl.BlockSpec((None, nb, ng * hb, n, n), lambda b, q, ch: (layer, b, q, 0, 0))
    return pl.pallas_call(
        functools.partial(_wkv_chunk_kernel, c=c, hb=hb, n=n, ng=ng, nb=nb),
        grid=(bsz // nb, n_heads // (hb * ng), t // c),
        in_specs=[seq_spec] * 6 + [st_spec, pl.BlockSpec(memory_space=pl.ANY)],
        out_specs=[seq_spec, st_out_spec],
        out_shape=[jax.ShapeDtypeStruct((bsz, t, d), F32),
                   jax.ShapeDtypeStruct(s_all.shape, F32)],
        scratch_shapes=[pltpu.VMEM((ng * nb, hb * n, hb * n), F32)],
        input_output_aliases={7: 1},
        compiler_params=_cparams(("parallel", "parallel", "arbitrary")),
        name="wkv_chunked",
    )(r, ld, k, v, kk, asig, s0, s_all)


def _head_sum(x, n):
    wd = min(MXU_WIDTH, x.shape[1])
    gi = lax.broadcasted_iota(jnp.int32, (wd, wd), 0) // n
    gj = lax.broadcasted_iota(jnp.int32, (wd, wd), 1) // n
    g = jnp.where(gi == gj, 1.0, 0.0).astype(BF16)
    hi = x.astype(BF16)
    lo = (x - hi.astype(F32)).astype(BF16)
    parts = []
    for s in range(0, x.shape[1], wd):
        parts.append(jnp.dot(hi[:, s:s + wd], g, preferred_element_type=F32)
                     + jnp.dot(lo[:, s:s + wd], g, preferred_element_type=F32))
    return parts[0] if len(parts) == 1 else jnp.concatenate(parts, axis=1)


def _softplus(z):
    return jnp.maximum(z, 0.0) + jnp.log(1.0 + jnp.exp(-jnp.abs(z)))


def _rwkv_pre_kernel(*refs, n, has_v_lora):
    if has_v_lora:
        (k_ref, v_ref, hw_ref, ha_ref, w2_ref, a2_ref, w0_ref, a0_ref, kk_ref, ka_ref,
         vf_ref, hv_ref, v2_ref, v0_ref, ld_out, a_out, kk_out, k_out, v_out) = refs
    else:
        (k_ref, v_ref, hw_ref, ha_ref, w2_ref, a2_ref, w0_ref, a0_ref, kk_ref, ka_ref,
         ld_out, a_out, kk_out, k_out) = refs
    k = k_ref[...]
    w_pre = w0_ref[...] + jnp.dot(jnp.tanh(hw_ref[...]).astype(BF16), w2_ref[...], preferred_element_type=F32)
    ld_out[...] = -jnp.exp(-_softplus(-w_pre) - 0.5)
    a = jax.nn.sigmoid(a0_ref[...] + jnp.dot(ha_ref[...].astype(BF16), a2_ref[...], preferred_element_type=F32))
    a_out[...] = a
    kk = k * kk_ref[...]
    kk_out[...] = kk / jnp.maximum(jnp.sqrt(_head_sum(kk * kk, n)), 1e-12)
    k_out[...] = k * (1.0 + (a - 1.0) * ka_ref[...])
    if has_v_lora:
        v = v_ref[...]
        gate = jax.nn.sigmoid(v0_ref[...] + jnp.dot(hv_ref[...].astype(BF16), v2_ref[...],
                                                    preferred_element_type=F32))
        v_out[...] = v + (vf_ref[...] - v) * gate


def rwkv_pre(k, v, hw, ha, p, v_first, hv, n):
    m, d = k.shape
    bm = _pick(m, (256, 128, 64, 32, 16, 8))
    has_v_lora = p["v_lora"] is not None
    row = lambda width: pl.BlockSpec((bm, width), lambda i: (i, 0))
    full = lambda a: pl.BlockSpec(a.shape, lambda i: (0, 0))
    vec = lambda z: z.reshape(1, d)
    args = [k, v, hw, ha, p["w2"], p["a2"], vec(p["w0"]), vec(p["a0"]), vec(p["k_k"]), vec(p["k_a"])]
    specs = [row(d), row(d), row(hw.shape[1]), row(ha.shape[1])] + [full(a) for a in args[4:]]
    n_out = 4
    if has_v_lora:
        v0, _, v2 = p["v_lora"]
        extra = [v_first, hv, v2, vec(v0)]
        args += extra
        specs += [row(d), row(hv.shape[1]), full(v2), full(extra[3])]
        n_out = 5
    outs = pl.pallas_call(
        functools.partial(_rwkv_pre_kernel, n=n, has_v_lora=has_v_lora),
        grid=(m // bm,),
        in_specs=specs,
        out_specs=[row(d)] * n_out,
        out_shape=[jax.ShapeDtypeStruct((m, d), F32)] * n_out,
        compiler_params=_cparams(("parallel",)),
        name="rwkv_pre",
    )(*args)
    return tuple(outs) + ((v,) if not has_v_lora else ())


def _rwkv_post_kernel(y_ref, r_ref, k_ref, v_ref, hg_ref, g2_ref, lnw_ref, lnb_ref, rk_ref, o_ref, *, n):
    y = y_ref[...]
    mu = _head_sum(y, n) / n
    dlt = y - mu
    var = _head_sum(dlt * dlt, n) / n
    yn = dlt * lax.rsqrt(var + n * GN_EPS_PER_CHANNEL) * lnw_ref[...] + lnb_ref[...]
    bonus = _head_sum(r_ref[...] * k_ref[...] * rk_ref[...], n) * v_ref[...]
    g = jnp.dot(jax.nn.sigmoid(hg_ref[...]).astype(BF16), g2_ref[...], preferred_element_type=F32)
    o_ref[...] = ((yn + bonus) * g).astype(o_ref.dtype)


def rwkv_post(y, r, k, v, hg, p, n):
    m, d = y.shape
    bm = _pick(m, (256, 128, 64, 32, 16, 8))
    row = lambda width: pl.BlockSpec((bm, width), lambda i: (i, 0))
    full = lambda a: pl.BlockSpec(a.shape, lambda i: (0, 0))
    vec = lambda z: z.reshape(1, d)
    consts = [p["g2"], vec(p["ln_w"]), vec(p["ln_b"]), vec(p["r_k"])]
    return pl.pallas_call(
        functools.partial(_rwkv_post_kernel, n=n),
        grid=(m // bm,),
        in_specs=[row(d)] * 4 + [row(hg.shape[1])] + [full(a) for a in consts],
        out_specs=row(d),
        out_shape=jax.ShapeDtypeStruct((m, d), BF16),
        compiler_params=_cparams(("parallel",)),
        name="rwkv_post",
    )(y, r, k, v, hg, *consts)


def _flash_kernel(qn_ref, qr_ref, kn_ref, kr_ref, v_ref, o_ref, *, bq, n_q, hp, dh, dv, scale):
    qi = pl.program_id(2)
    dn = (((1,), (1,)), ((), ()))
    heads = range(hp)
    hs = [slice(h * dh, (h + 1) * dh) for h in heads]
    vs = [slice(h * dv, (h + 1) * dv) for h in heads]

    for i in range(n_q):
        @pl.when(qi == i)
        def _(i=i):
            kv_len = (i + 1) * bq
            past = kv_len - bq
            tri = (lax.broadcasted_iota(jnp.int32, (bq, bq), 1) <= lax.broadcasted_iota(jnp.int32, (bq, bq), 0))
            s = [(lax.dot_general(qn_ref[0, :, hs[h]], kn_ref[0, :kv_len, hs[h]], dn, preferred_element_type=F32)
                  + lax.dot_general(qr_ref[0, :, hs[h]], kr_ref[0, :kv_len, hs[h]], dn,
                                    preferred_element_type=F32)) * scale for h in heads]
            s_diag = [jnp.where(tri, s[h][:, past:], -jnp.inf) for h in heads]
            m = [jnp.max(s_diag[h], axis=-1, keepdims=True) for h in heads]
            if past:
                m = [jnp.maximum(m[h], jnp.max(s[h][:, :past], axis=-1, keepdims=True)) for h in heads]
            p_diag = [jnp.exp(s_diag[h] - m[h]) for h in heads]
            l = [jnp.sum(p_diag[h], axis=-1, keepdims=True) for h in heads]
            o = [jnp.dot(p_diag[h].astype(BF16), v_ref[0, past:kv_len, vs[h]], preferred_element_type=F32)
                 for h in heads]
            if past:
                p_past = [jnp.exp(s[h][:, :past] - m[h]) for h in heads]
                l = [l[h] + jnp.sum(p_past[h], axis=-1, keepdims=True) for h in heads]
                o = [o[h] + jnp.dot(p_past[h].astype(BF16), v_ref[0, :past, vs[h]], preferred_element_type=F32)
                     for h in heads]
            for h in heads:
                o_ref[0, :, vs[h]] = (o[h] / l[h]).astype(o_ref.dtype)


def flash_prompt(qn, qr, kn, kr, v, n_heads, scale):
    bsz, s_len, _ = qn.shape
    dh = qn.shape[-1] // n_heads
    dv = v.shape[-1] // n_heads
    bq = _pick(s_len, (512, 256, 128, 64, 32, 16, 8))
    n_q = s_len // bq
    hp = _pick(n_heads, (2, 1))
    q_spec = pl.BlockSpec((1, bq, hp * dh), lambda b, h, i: (b, i, h))
    k_spec = pl.BlockSpec((1, s_len, hp * dh), lambda b, h, i: (b, 0, h))
    return pl.pallas_call(
        functools.partial(_flash_kernel, bq=bq, n_q=n_q, hp=hp, dh=dh, dv=dv, scale=scale),
        grid=(bsz, n_heads // hp, n_q),
        in_specs=[q_spec, q_spec, k_spec, k_spec,
                  pl.BlockSpec((1, s_len, hp * dv), lambda b, h, i: (b, 0, h))],
        out_specs=pl.BlockSpec((1, bq, hp * dv), lambda b, h, i: (b, i, h)),
        out_shape=jax.ShapeDtypeStruct((bsz, s_len, n_heads * dv), BF16),
        compiler_params=_cparams(("parallel", "parallel", "arbitrary")),
        name="flash_prompt",
    )(qn, qr, kn, kr, v)


def _mla_head_kernel(*refs, n_heads, d_qk, half, is_query):
    tile = lambda z: jnp.concatenate([z] * n_heads, axis=1)
    if is_query:
        xn_ref, xr_ref, cos_ref, sin_a_ref, sin_b_ref, gn_ref, gr_ref, on_ref, or_ref = refs
        xr = xr_ref[...]
        width = xr.shape[1]
        xr = (xr * tile(cos_ref[...]) + pltpu.roll(xr, width - half, axis=1) * tile(sin_a_ref[...])
              + pltpu.roll(xr, half, axis=1) * tile(sin_b_ref[...]))
    else:
        xn_ref, kpe_ref, gn_ref, gr_ref, on_ref, or_ref = refs
        kpe = kpe_ref[...]
        pad = jnp.zeros((kpe.shape[0], xn_ref.shape[1] // n_heads - kpe.shape[1]), F32)
        xr = tile(jnp.concatenate([kpe, pad], axis=1))
    xn = xn_ref[...]
    inv = lax.rsqrt(_head_sum(xn * xn + xr * xr, xn.shape[1] // n_heads) / d_qk + RMS_EPS)
    on_ref[...] = (xn * inv * gn_ref[...]).astype(on_ref.dtype)
    or_ref[...] = (xr * inv * gr_ref[...]).astype(or_ref.dtype)


def mla_head_norm(x, gain_n, gain_r, n_heads, d_qk, d_rope, rope_tabs=None, k_pe=None):
    m = x.shape[0]
    is_query = rope_tabs is not None
    wd = gain_n.shape[1]
    dh = wd // n_heads
    bm = _pick(m, (256, 128, 64, 32, 16, 8))
    row = lambda width, blk=0: pl.BlockSpec((bm, width), lambda i, blk=blk: (i, blk))
    full = lambda a: pl.BlockSpec(a.shape, lambda i: (0, 0))
    if is_query:
        args = [x, x, *rope_tabs, gain_n, gain_r]
        specs = [row(wd, 0), row(wd, 1), row(dh), row(dh), row(dh), full(gain_n), full(gain_r)]
    else:
        args = [x, k_pe, gain_n, gain_r]
        specs = [row(wd), row(k_pe.shape[1]), full(gain_n), full(gain_r)]
    return pl.pallas_call(
        functools.partial(_mla_head_kernel, n_heads=n_heads, d_qk=d_qk, half=d_rope // 2, is_query=is_query),
        grid=(m // bm,),
        in_specs=specs,
        out_specs=[row(wd), row(wd)],
        out_shape=[jax.ShapeDtypeStruct((m, wd), BF16)] * 2,
        compiler_params=_cparams(("parallel",)),
        name="mla_head_norm",
    )(*args)


def _kscale_kernel(pt_ref, *refs, npg, n_heads, d_qk):
    del pt_ref
    c_refs = refs[:npg]
    kpe_refs = refs[npg:2 * npg]
    wt_ref = refs[2 * npg]
    s_ref, c_out, kpe_out = refs[2 * npg + 1:]
    page = c_refs[0].shape[1]
    pair = 2 if npg % 2 == 0 else 1
    for p in range(0, npg, pair):
        keys = slice(p * page, (p + pair) * page)
        c = jnp.concatenate([c_refs[p + q][0] for q in range(pair)], axis=0).astype(BF16)
        kpe_t = jnp.concatenate([kpe_refs[p + q][0] for q in range(pair)], axis=1)
        k_t = lax.dot_general(wt_ref[...], c, (((1,), (1,)), ((), ())), preferred_element_type=F32)
        sq = k_t * k_t
        ss = jnp.sum(sq.reshape(n_heads, sq.shape[0] // n_heads, sq.shape[1]), axis=1)
        pe = jnp.sum(kpe_t * kpe_t, axis=0, keepdims=True)
        s_ref[0, :, keys] = lax.rsqrt((ss + pe) / d_qk + RMS_EPS)
        c_out[0, keys, :] = c
        kpe_out[0, :, keys] = kpe_t.astype(BF16)


def paged_key_scale(cache_c, cache_kpe_t, page_table_flat, w_uk_t, n_heads, d_qk, npg):
    n_used = page_table_flat.shape[0]
    page, d_c = cache_c.shape[1:]
    d_r = cache_kpe_t.shape[1]
    n_grp = n_used // npg
    nk = npg * page

    def page_spec(shape, p):
        return pl.BlockSpec((1,) + shape, lambda g, pt, p=p: (pt[g * npg + p], 0, 0))

    in_specs = ([page_spec((page, d_c), p) for p in range(npg)] + [page_spec((d_r, page), p) for p in range(npg)]
                + [pl.BlockSpec(w_uk_t.shape, lambda g, pt: (0, 0))])
    out = lambda shape: pl.BlockSpec((1,) + shape, lambda g, pt: (g, 0, 0))
    grid_spec = pltpu.PrefetchScalarGridSpec(
        num_scalar_prefetch=1, grid=(n_grp,), in_specs=in_specs,
        out_specs=[out((n_heads, nk)), out((nk, d_c)), out((d_r, nk))])
    return pl.pallas_call(
        functools.partial(_kscale_kernel, npg=npg, n_heads=n_heads, d_qk=d_qk),
        grid_spec=grid_spec,
        out_shape=[jax.ShapeDtypeStruct((n_grp, n_heads, nk), F32),
                   jax.ShapeDtypeStruct((n_grp, nk, d_c), BF16),
                   jax.ShapeDtypeStruct((n_grp, d_r, nk), BF16)],
        compiler_params=_cparams(("arbitrary",)),
        name="paged_key_scale",
    )(page_table_flat, *([cache_c] * npg), *([cache_kpe_t] * npg), w_uk_t)


def _paged_attn_kernel(ql_ref, qr_ref, cnew_ref, pnew_ref, snew_ref, c_ref, p_ref, s_ref, o_ref,
                       m_sc, l_sc, acc_sc, *, bb, n_heads, t_len, scale):
    g = pl.program_id(1)
    rows = n_heads * t_len
    dn = (((1,), (1,)), ((), ()))

    def process(c, kpe_t, key_scale, mask):
        bs = range(bb)
        sc = [lax.dot_general(ql_ref[b], c[b], dn, preferred_element_type=F32)
              + jnp.dot(qr_ref[b], kpe_t[b], preferred_element_type=F32) for b in bs]
        sc = [sc[b] * (jnp.concatenate([key_scale[b]] * t_len, axis=0) * scale) for b in bs]
        if mask is not None:
            sc = [jnp.where(mask, sc[b], -jnp.inf) for b in bs]
        m_new = [jnp.maximum(m_sc[b], jnp.max(sc[b], axis=-1, keepdims=True)) for b in bs]
        alpha = [jnp.exp(m_sc[b] - m_new[b]) for b in bs]
        p = [jnp.exp(sc[b] - m_new[b]) for b in bs]
        pv = [jnp.dot(p[b].astype(BF16), c[b], preferred_element_type=F32) for b in bs]
        for b in bs:
            l_sc[b] = l_sc[b] * alpha[b] + jnp.sum(p[b], axis=-1, keepdims=True)
            acc_sc[b] = acc_sc[b] * alpha[b] + pv[b]
            m_sc[b] = m_new[b]

    @pl.when(g == 0)
    def _():
        m_sc[...] = jnp.full(m_sc.shape, -jnp.inf, F32)
        l_sc[...] = jnp.zeros(l_sc.shape, F32)
        acc_sc[...] = jnp.zeros(acc_sc.shape, F32)
        nk = cnew_ref.shape[1]
        t_row = lax.broadcasted_iota(jnp.int32, (rows, nk), 0) // n_heads
        key = lax.broadcasted_iota(jnp.int32, (rows, nk), 1)
        process([cnew_ref[b] for b in range(bb)], [pnew_ref[b] for b in range(bb)],
                [snew_ref[b] for b in range(bb)], key <= t_row)

    process([c_ref[b, 0] for b in range(bb)], [p_ref[b, 0] for b in range(bb)],
            [s_ref[b, 0] for b in range(bb)], None)

    @pl.when(g == pl.num_programs(1) - 1)
    def _():
        for b in range(bb):
            o_ref[b] = acc_sc[b] / l_sc[b]


def paged_attention(q_lat, q_r, c_new, kpe_new_t, s_new, c_past, kpe_past_t, s_past, n_heads, t_len, scale):
    bsz, rows, d_c = q_lat.shape
    d_r = q_r.shape[2]
    _, n_grp, nk, _ = c_past.shape
    kn = c_new.shape[1]
    bb = _pick(bsz, (2, 1))

    def per_b(shape):
        return pl.BlockSpec((bb,) + shape, lambda b, g: (b, 0, 0))

    def per_g(shape):
        return pl.BlockSpec((bb, 1) + shape, lambda b, g: (b, g, 0, 0))

    return pl.pallas_call(
        functools.partial(_paged_attn_kernel, bb=bb, n_heads=n_heads, t_len=t_len, scale=scale),
        grid=(bsz // bb, n_grp),
        in_specs=[per_b((rows, d_c)), per_b((rows, d_r)), per_b((kn, d_c)), per_b((d_r, kn)),
                  per_b((n_heads, kn)), per_g((nk, d_c)), per_g((d_r, nk)), per_g((n_heads, nk))],
        out_specs=per_b((rows, d_c)),
        out_shape=jax.ShapeDtypeStruct((bsz, rows, d_c), F32),
        scratch_shapes=[pltpu.VMEM((bb, rows, 1), F32), pltpu.VMEM((bb, rows, 1), F32),
                        pltpu.VMEM((bb, rows, d_c), F32)],
        compiler_params=_cparams(("parallel", "arbitrary")),
        name="paged_attention",
    )(q_lat, q_r, c_new, kpe_new_t, s_new, c_past, kpe_past_t, s_past)


def _rms_norm(x, g):
    return x * lax.rsqrt(jnp.mean(x * x, axis=-1, keepdims=True) + RMS_EPS) * g


def _rope(x, pos):
    half = x.shape[-1] // 2
    inv = ROPE_BASE ** (-jnp.arange(half, dtype=F32) / half)
    ang = pos.astype(F32)[:, None] * inv[None, :]
    cos = jnp.cos(ang)[None, :, None, :]
    sin = jnp.sin(ang)[None, :, None, :]
    x1, x2 = x[..., :half], x[..., half:]
    return jnp.concatenate([x1 * cos - x2 * sin, x1 * sin + x2 * cos], axis=-1)


def _conv_ffn(x, prev, p):
    hid, state = ffn_in(x, p["norm"], prev, p["w_gate"], p["w_up"], p["conv_w"], p["conv_b"])
    return mm(hid, p["w_down"], res=x.reshape(hid.shape[0], -1)).reshape(x.shape), state


def _rwkv7_time_mix(x, shift_prev, wkv_prev, v_first, p, wkv_all, layer):
    bsz, t, d = x.shape
    n_heads, n = p["r_k"].shape
    h = _rms_norm(x, p["norm"])
    prev = jnp.concatenate([shift_prev[:, None, :], h[:, :-1]], axis=1)
    xx = prev - h
    xr, xw, xk, xv, xa, xg = ((h + xx * p["mix"][i]).astype(BF16) for i in range(6))
    flat = lambda z: z.reshape(bsz * t, -1)
    r = mm(flat(xr), p["wr"])
    k = mm(flat(xk), p["wk"])
    v = mm(flat(xv), p["wv"])
    hv = None if p["v_lora"] is None else mm(flat(xv), p["v_lora"][1])
    log_decay, a, kk, k, v = rwkv_pre(k, v, mm(flat(xw), p["w1"]), mm(flat(xa), p["a1"]), p, v_first, hv, n)
    if p["v_lora"] is None:
        v_first = v
    seq = lambda z: z.reshape(bsz, t, d)
    y, s = wkv_chunked(seq(r), seq(log_decay), seq(k), seq(v), seq(kk), seq(a), wkv_prev, n_heads,
                       wkv_all, layer)
    out = rwkv_post(flat(y), r, k, v, mm(flat(xg), p["g1"]), p, n)
    return mm(out, p["wo"], res=flat(x)).reshape(x.shape), h[:, -1], s, v_first


def _mla_kv_side(x, pos, kv):
    u = _rms_norm(x, kv["norm"]).astype(BF16)
    d_c = kv["norm_c"].shape[0]
    ckr = mm(u, kv["w_dkv_kr"])
    c = _rms_norm(ckr[..., :d_c], kv["norm_c"])
    k_pe = _rope(ckr[..., d_c:][:, :, None, :], pos)[:, :, 0, :]
    return c, k_pe


def _head_gains(g, n_heads, d_nope, other=None):
    if other is not None:
        g = g * other
    g_r = jnp.pad(g[d_nope:], (0, 2 * d_nope - g.shape[0]))
    return jnp.tile(g[:d_nope], n_heads)[None], jnp.tile(g_r, n_heads)[None]


def _mla_query(x, pos, p, kv, n_heads, d_nope, d_rope, absorb_key_gain):
    bsz, t, _ = x.shape
    h = _rms_norm(x, p["norm"]).astype(BF16)
    cq = _rms_norm(mm(h, p["w_dq"]), p["norm_cq"]).astype(BF16)
    q = mm(cq.reshape(bsz * t, -1), p["w_uq"])
    half = d_rope // 2
    inv = ROPE_BASE ** (-jnp.arange(half, dtype=F32) / half)
    ang = pos.astype(F32)[:, None] * inv[None, :]
    cos, sin, zero = jnp.cos(ang), jnp.sin(ang), jnp.zeros((t, half), F32)
    lane_pad = jnp.zeros((t, d_nope - d_rope), F32)
    tabs = tuple(jnp.tile(jnp.concatenate(parts + [lane_pad], axis=1), (bsz, 1))
                 for parts in ([cos, cos], [-sin, zero], [zero, sin]))
    g_n, g_r = _head_gains(p["norm_q"], n_heads, d_nope, kv["norm_k"] if absorb_key_gain else None)
    return mla_head_norm(q, g_n, g_r, n_heads, d_nope + d_rope, d_rope, rope_tabs=tabs)


def _prompt_attend_fn(c, k_pe, kv, n_heads, d_nope):
    bsz, s_len, _ = c.shape
    d_rope = k_pe.shape[-1]
    d_qk = d_nope + d_rope
    g_n, g_r = _head_gains(kv["norm_k"], n_heads, d_nope)
    k_n, k_r = mla_head_norm(mm(c.reshape(bsz * s_len, -1), kv["w_uk_2d"]), g_n, g_r, n_heads, d_qk, d_rope,
                             k_pe=k_pe.reshape(bsz * s_len, d_rope))
    v = mm(c, kv["w_uv_2d"], BF16)
    seq = lambda z: z.reshape(bsz, s_len, -1)

    def attend(q):
        return flash_prompt(seq(q[0]), seq(q[1]), seq(k_n), seq(k_r), v, n_heads, d_qk ** -0.5)

    return attend


def _sample_attend_fn(c_new, kpe_new, cache_c, cache_kpe, page_table, kv, n_heads, d_nope):
    bsz, t, d_c = c_new.shape
    d_r = kpe_new.shape[-1]
    d_qk = d_nope + d_r
    page = cache_c.shape[1]
    pt_flat = page_table.reshape(-1)
    npg = _pick(page_table.shape[1], (PAGES_PER_STEP, 8, 4, 2, 1))

    k_nope_new = mm(c_new, kv["w_uk_2d"]).reshape(bsz, t, n_heads, d_nope)
    ss = jnp.sum(k_nope_new * k_nope_new, axis=-1) + jnp.sum(kpe_new * kpe_new, axis=-1)[..., None]
    s_new = lax.rsqrt(ss / d_qk + RMS_EPS)
    kn = page
    c_new_p = jnp.pad(c_new.astype(BF16), ((0, 0), (0, kn - t), (0, 0)))
    kpe_new_t = jnp.pad(kpe_new.astype(BF16).transpose(0, 2, 1), ((0, 0), (0, 0), (0, kn - t)))
    s_new_p = jnp.pad(s_new.transpose(0, 2, 1), ((0, 0), (0, 0), (0, kn - t)), constant_values=1.0)

    s_past, c_past, kpe_past_t = paged_key_scale(cache_c, cache_kpe.transpose(0, 2, 1), pt_flat, kv["w_uk_t"],
                                                 n_heads, d_qk, npg)
    n_grp = page_table.shape[1] // npg
    nk = npg * page
    s_past = s_past.reshape(bsz, n_grp, n_heads, nk)
    c_past = c_past.reshape(bsz, n_grp, nk, d_c)
    kpe_past_t = kpe_past_t.reshape(bsz, n_grp, d_r, nk)
    scale = d_qk ** -0.5

    def attend(q):
        q_lat = hmm(q[0], kv["w_uk_hdc"], BF16).reshape(bsz, t * n_heads, d_c)
        q_r = q[1].reshape(bsz, t * n_heads, d_nope)[..., :d_r]
        o_lat = paged_attention(q_lat, q_r, c_new_p, kpe_new_t, s_new_p, c_past, kpe_past_t, s_past,
                                n_heads, t, scale)
        o = hmm(o_lat.reshape(bsz * t, n_heads * d_c), kv["w_uv_hcd"])
        return o.reshape(bsz, t, -1)

    return attend


def kernel(x_prompt, x_sample, cache_kv_latent, cache_k_pe, state_wkv, state_shift, state_conv, page_table,
           ffn_norm, ffn_w_gate, ffn_w_up, ffn_conv_w, ffn_conv_b, ffn_w_down,
           rw_norm, rw_mix, rw_w0, rw_w1, rw_w2, rw_a0, rw_a1, rw_a2, rw_v0, rw_v1, rw_v2,
           rw_g1, rw_g2, rw_k_k, rw_k_a, rw_r_k, rw_wr, rw_wk, rw_wv, rw_ln_w, rw_ln_b, rw_wo,
           kv_norm, kv_w_dkv, kv_norm_c, kv_w_kr, kv_w_uk, kv_w_uv, kv_norm_k,
           mla_norm, mla_w_dq, mla_norm_cq, mla_w_uq, mla_norm_q, mla_wo):
    depth = ffn_norm.shape[0]
    n_a = rw_norm.shape[0]
    d_c, n_heads, d_nope = kv_w_uk.shape
    d_v = kv_w_uv.shape[2]
    bf = lambda z: z.astype(BF16)

    ffn = [dict(norm=ffn_norm[l], w_gate=bf(ffn_w_gate[l]), w_up=bf(ffn_w_up[l]), conv_w=ffn_conv_w[l],
                conv_b=ffn_conv_b[l], w_down=bf(ffn_w_down[l])) for l in range(depth)]
    rw = [dict(norm=rw_norm[l], mix=rw_mix[l], w0=rw_w0[l], w1=bf(rw_w1[l]), w2=bf(rw_w2[l]),
               a0=rw_a0[l], a1=bf(rw_a1[l]), a2=bf(rw_a2[l]), g1=bf(rw_g1[l]), g2=bf(rw_g2[l]),
               k_k=rw_k_k[l], k_a=rw_k_a[l], r_k=rw_r_k[l], wr=bf(rw_wr[l]), wk=bf(rw_wk[l]),
               wv=bf(rw_wv[l]), ln_w=rw_ln_w[l], ln_b=rw_ln_b[l], wo=bf(rw_wo[l]),
               v_lora=None if l == 0 else (rw_v0[l - 1], bf(rw_v1[l - 1]), bf(rw_v2[l - 1])))
          for l in range(n_a)]
    kv = dict(norm=kv_norm, norm_c=kv_norm_c, norm_k=kv_norm_k,
              w_dkv_kr=bf(jnp.concatenate([kv_w_dkv, kv_w_kr], axis=1)),
              w_uk_2d=bf(kv_w_uk.reshape(d_c, n_heads * d_nope)),
              w_uv_2d=bf(kv_w_uv.reshape(d_c, n_heads * d_v)),
              w_uk_t=bf(kv_w_uk.reshape(d_c, n_heads * d_nope).T),
              w_uk_hdc=bf(kv_w_uk.transpose(1, 2, 0)),
              w_uv_hcd=bf(kv_w_uv.transpose(1, 0, 2)))
    d_qc = mla_w_uq.shape[1]
    d_rope = kv_w_kr.shape[1]

    def split_heads(w_uq):
        w_r = jnp.pad(w_uq[:, :, d_nope:], ((0, 0), (0, 0), (0, d_nope - d_rope)))
        return jnp.concatenate([w_uq[:, :, :d_nope].reshape(d_qc, -1), w_r.reshape(d_qc, -1)], axis=1)

    mla = [dict(norm=mla_norm[b], w_dq=bf(mla_w_dq[b]), norm_cq=mla_norm_cq[b],
                w_uq=bf(split_heads(mla_w_uq[b])), norm_q=mla_norm_q[b],
                wo=bf(mla_wo[b])) for b in range(depth - n_a)]

    def trunk(x, pos, shift_in, wkv_in, conv_in, make_attend, absorb_key_gain):
        v_first = attend = c = k_pe = None
        shifts, convs = [], []
        wkvs = jnp.zeros(wkv_in.shape, F32)
        for layer in range(depth):
            if layer < n_a:
                x, sh, wkvs, v_first = _rwkv7_time_mix(x, shift_in[layer], wkv_in[layer], v_first, rw[layer],
                                                       wkvs, layer)
                shifts.append(sh)
            else:
                if layer == n_a:
                    c, k_pe = _mla_kv_side(x, pos, kv)
                    attend = make_attend(c, k_pe)
                p = mla[layer - n_a]
                q = _mla_query(x, pos, p, kv, n_heads, d_nope, d_rope, absorb_key_gain)
                x = mm(attend(q), p["wo"], res=x)
            x, cv = _conv_ffn(x, conv_in[layer], ffn[layer])
            convs.append(cv)
        return x, c, k_pe, wkvs, jnp.stack(shifts), jnp.stack(convs)

    bp, sp, d = x_prompt.shape
    n_rw_heads, n = rw_r_k.shape[1:]
    d_ff = ffn_w_gate.shape[2]
    out_p = trunk(
        x_prompt, jnp.arange(sp),
        jnp.zeros((n_a, bp, d), F32), jnp.zeros((n_a, bp, n_rw_heads, n, n), F32),
        jnp.zeros((depth, bp, CONV_W - 1, d_ff), F32),
        lambda c, kp: _prompt_attend_fn(c, kp, kv, n_heads, d_nope), False)

    past_len = page_table.shape[1] * cache_kv_latent.shape[1]
    out_s = trunk(
        x_sample, past_len + jnp.arange(x_sample.shape[1]), state_shift, state_wkv, state_conv,
        lambda c, kp: _sample_attend_fn(c, kp, cache_kv_latent, cache_k_pe, page_table, kv, n_heads, d_nope),
        True)

    return (out_p[0], out_s[0]) + tuple(out_p[1:]) + tuple(out_s[1:])
```

```python
import functools

import jax
import jax.numpy as jnp
from jax import lax
from jax.experimental import pallas as pl
from jax.experimental.pallas import tpu as pltpu

F32 = jnp.float32
BF16 = jnp.bfloat16

RMS_EPS = 1e-6
ROPE_BASE = 10000.0
GN_EPS_PER_CHANNEL = 1e-5
CONV_W = 3

VMEM_LIMIT_BYTES = 52 * 1024 * 1024
MM_VMEM_BUDGET_BYTES = 44 * 1024 * 1024
LANES = 128
MXU_WIDTH = 256
PAGES_PER_STEP = 32
ROW_BLOCK = 1024
WKV_CHUNK = 64
_ROW_BLOCKS = (1024, 512, 256, 128, 64, 32, 16, 8)


def _pick(n, candidates):
    for c in candidates:
        if c <= n and n % c == 0:
            return c
    return n


def _cparams(sem):
    return pltpu.CompilerParams(dimension_semantics=sem, vmem_limit_bytes=VMEM_LIMIT_BYTES)


def _mm_kernel(*refs, has_norm, has_res):
    x_ref, w_ref = refs[:2]
    rest = list(refs[2:])
    x = x_ref[...]
    if has_norm:
        x = x * lax.rsqrt(jnp.mean(x * x, axis=-1, keepdims=True) + RMS_EPS) * rest.pop(0)[...]
    acc = jnp.dot(x.astype(BF16), w_ref[...], preferred_element_type=F32)
    if has_res:
        acc = rest.pop(0)[...] + acc
    o_ref = rest.pop(0)
    o_ref[...] = acc.astype(o_ref.dtype)


def _mm_col_block(bm, k, n, x_bytes, out_bytes):
    cands = [n] if n <= 1024 else [c for c in (1024, 512, 256, 128) if n % c == 0]
    for bn in cands:
        if 2 * (bm * k * x_bytes + k * bn * 2 + bm * bn * out_bytes) <= MM_VMEM_BUDGET_BYTES:
            return bn
    return cands[-1]


def mm(x, w, out_dtype=F32, res=None, norm_gain=None):
    lead = x.shape[:-1]
    k, n = w.shape
    x2 = x.reshape(-1, k)
    m = x2.shape[0]
    bm = _pick(m, _ROW_BLOCKS)
    out_bytes = jnp.dtype(out_dtype).itemsize * (1 if res is None else 2)
    bn = _mm_col_block(bm, k, n, x2.dtype.itemsize, out_bytes)
    out_spec = pl.BlockSpec((bm, bn), lambda i, j: (i, j))
    args = [x2, w]
    in_specs = [pl.BlockSpec((bm, k), lambda i, j: (i, 0)), pl.BlockSpec((k, bn), lambda i, j: (0, j))]
    if norm_gain is not None:
        args.append(norm_gain.reshape(1, k))
        in_specs.append(pl.BlockSpec((1, k), lambda i, j: (0, 0)))
    if res is not None:
        args.append(res.reshape(m, n))
        in_specs.append(out_spec)
    out = pl.pallas_call(
        functools.partial(_mm_kernel, has_norm=norm_gain is not None, has_res=res is not None),
        grid=(m // bm, n // bn),
        in_specs=in_specs,
        out_specs=out_spec,
        out_shape=jax.ShapeDtypeStruct((m, n), out_dtype),
        compiler_params=_cparams(("parallel", "arbitrary")),
        name="mm",
    )(*args)
    return out.reshape(lead + (n,))


def _hmm_kernel(x_ref, w_ref, o_ref):
    o_ref[...] = jnp.dot(x_ref[...].astype(BF16), w_ref[0], preferred_element_type=F32).astype(o_ref.dtype)


def hmm(x, w, out_dtype=F32):
    m = x.shape[0]
    n_heads, k, n = w.shape
    bm = _pick(m, _ROW_BLOCKS)
    return pl.pallas_call(
        _hmm_kernel,
        grid=(n_heads, m // bm),
        in_specs=[pl.BlockSpec((bm, k), lambda h, i: (i, h)),
                  pl.BlockSpec((1, k, n), lambda h, i: (h, 0, 0))],
        out_specs=pl.BlockSpec((bm, n), lambda h, i: (i, h)),
        out_shape=jax.ShapeDtypeStruct((m, n_heads * n), out_dtype),
        compiler_params=_cparams(("parallel", "parallel")),
        name="hmm",
    )(x, w)


def _ffn_in_kernel(*refs, t_len, bm, seq_in_block):
    if seq_in_block:
        x_ref, g_ref, wg_ref, wu_ref, cw_ref, cb_ref, e_ref, hid_ref, tail_ref, h_sc = refs
    else:
        x_ref, xp_ref, g_ref, wg_ref, wu_ref, cw_ref, cb_ref, e_ref, hid_ref, tail_ref, h_sc = refs

    def norm(z):
        return (z * lax.rsqrt(jnp.mean(z * z, axis=-1, keepdims=True) + RMS_EPS) * g_ref[...]).astype(BF16)

    @pl.when(pl.program_id(1) == 0)
    def _():
        h_sc[...] = norm(x_ref[...])

    x = h_sc[...]
    wg = wg_ref[...]
    u = jnp.dot(x, wg, preferred_element_type=F32)
    up = jnp.dot(x, wu_ref[...], preferred_element_type=F32)
    row = lax.broadcasted_iota(jnp.int32, u.shape, 0)
    if seq_in_block:
        t = row % t_len
        u1 = jnp.where(t < 1, e_ref[0], pltpu.roll(u, 1, axis=0))
        u2 = jnp.where(t < 2, e_ref[1], pltpu.roll(u, 2, axis=0))
        tail_ref[...] = u
    else:
        prev8 = jnp.dot(norm(xp_ref[...]), wg, preferred_element_type=F32)
        seq_start = (pl.program_id(0) * bm) % t_len == 0
        prev8 = jnp.where(seq_start, e_ref[0], prev8)
        u1 = jnp.where(row < 1, prev8[7:8], pltpu.roll(u, 1, axis=0))
        u2 = jnp.where(row < 1, prev8[6:7], jnp.where(row < 2, prev8[7:8], pltpu.roll(u, 2, axis=0)))
        tail_ref[0] = u[bm - 8:bm]
    conv = cb_ref[...] + cw_ref[0:1] * u2 + cw_ref[1:2] * u1 + cw_ref[2:3] * u
    hid_ref[...] = (conv * jax.nn.sigmoid(conv) * up).astype(hid_ref.dtype)


def ffn_in(x, gain, prev, w_gate, w_up, conv_w, conv_b):
    bsz, t_len, d = x.shape
    f = w_gate.shape[1]
    m = bsz * t_len
    x = x.reshape(m, d)
    bn = _pick(f, (512, 256, 128))
    seq_in_block = t_len < ROW_BLOCK
    if seq_in_block:
        bm = _pick(m, tuple(c for c in _ROW_BLOCKS if c <= ROW_BLOCK and c % t_len == 0))
        tpos = jnp.arange(t_len)[None, :, None]
        e = jnp.stack([jnp.broadcast_to(prev[:, 1:2], (bsz, t_len, f)),
                       jnp.where(tpos == 0, prev[:, 0:1], prev[:, 1:2])]).reshape(2, m, f)
        lead = []
        lead_specs = []
        e_spec = pl.BlockSpec((2, bm, bn), lambda i, j: (0, i, j))
        tail_shape = jax.ShapeDtypeStruct((m, f), F32)
        tail_spec = pl.BlockSpec((bm, bn), lambda i, j: (i, j))
    else:
        bm = _pick(t_len, tuple(c for c in _ROW_BLOCKS if c <= ROW_BLOCK))
        e = jnp.pad(prev, ((0, 0), (6, 0), (0, 0)))
        lead = [x]
        lead_specs = [pl.BlockSpec((8, d), lambda i, j: (jnp.maximum(i * (bm // 8) - 1, 0), 0))]
        e_spec = pl.BlockSpec((1, 8, bn), lambda i, j: ((i * bm) // t_len, 0, j))
        tail_shape = jax.ShapeDtypeStruct((m // bm, 8, f), F32)
        tail_spec = pl.BlockSpec((1, 8, bn), lambda i, j: (i, 0, j))
    w_spec = pl.BlockSpec((d, bn), lambda i, j: (0, j))
    hid, tail = pl.pallas_call(
        functools.partial(_ffn_in_kernel, t_len=t_len, bm=bm, seq_in_block=seq_in_block),
        grid=(m // bm, f // bn),
        in_specs=([pl.BlockSpec((bm, d), lambda i, j: (i, 0))] + lead_specs
                  + [pl.BlockSpec((1, d), lambda i, j: (0, 0)),
                     w_spec, w_spec, pl.BlockSpec((CONV_W, bn), lambda i, j: (0, j)),
                     pl.BlockSpec((1, bn), lambda i, j: (0, j)), e_spec]),
        out_specs=[pl.BlockSpec((bm, bn), lambda i, j: (i, j)), tail_spec],
        out_shape=[jax.ShapeDtypeStruct((m, f), BF16), tail_shape],
        scratch_shapes=[pltpu.VMEM((bm, d), BF16)],
        compiler_params=_cparams(("parallel", "arbitrary")),
        name="ffn_in",
    )(x, *lead, gain.reshape(1, d), w_gate, w_up, conv_w, conv_b.reshape(1, f), e)
    if seq_in_block:
        state = tail.reshape(bsz, t_len, f)[:, t_len - 2:]
    else:
        state = tail.reshape(bsz, t_len // bm, 8, f)[:, -1, 6:]
    return hid, state


def _dot(a, b, dims=((1,), (0,))):
    return lax.dot_general(a.astype(BF16), b.astype(BF16), (dims, ((), ())), preferred_element_type=F32)


_NT = ((1,), (1,))
_TN = ((0,), (0,))


def _wkv_chunk_kernel(r_ref, ld_ref, k_ref, v_ref, kk_ref, as_ref, s0_ref, sall_ref, y_ref, sout_ref, s_sc, *,
                      c, hb, n, ng, nb):
    del sall_ref
    w = hb * n
    rb = hb * c
    rows = nb * rb
    first = pl.program_id(2) == 0
    last = pl.program_id(2) == pl.num_programs(2) - 1
    ri = lax.broadcasted_iota(jnp.int32, (rb, w), 0)
    ci = lax.broadcasted_iota(jnp.int32, (rb, w), 1)
    head_mask = (ri // c) == (ci // n)
    bd = lambda z: jnp.where(head_mask, jnp.concatenate([z] * hb, axis=0), 0.0)
    trow = lax.broadcasted_iota(jnp.int32, (c, w), 0)
    tr = lax.broadcasted_iota(jnp.int32, (rows, rows), 0)
    tc_ = lax.broadcasted_iota(jnp.int32, (rows, rows), 1)
    same = (tr // c) == (tc_ // c)
    strict = same & ((tr % c) > (tc_ % c))
    incl = same & ((tr % c) >= (tc_ % c))
    eye = jnp.where(tr == tc_, 1.0, 0.0)
    grp = range(ng)
    bat = range(nb)
    lanes = [slice(gi * w, (gi + 1) * w) for gi in grp]
    cat = lambda parts: parts[0] if len(parts) == 1 else jnp.concatenate(parts, axis=0)
    rsl = lambda z, bi: z[bi * rb:(bi + 1) * rb]

    @pl.when(first)
    def _():
        sr = lax.broadcasted_iota(jnp.int32, (w, w), 0)
        sc_ = lax.broadcasted_iota(jnp.int32, (w, w), 1)
        for gi in grp:
            for bi in bat:
                s0 = jnp.concatenate([s0_ref[bi, gi * hb + h] for h in range(hb)], axis=0)
                s_sc[gi * nb + bi] = jnp.where((sr // n) == (sc_ // n),
                                               jnp.concatenate([s0] * hb, axis=1), 0.0)

    def prep(gi):
        parts = []
        for bi in bat:
            r, ld, k, v, kk, asig = (ref[bi, :, lanes[gi]]
                                     for ref in (r_ref, ld_ref, k_ref, v_ref, kk_ref, as_ref))
            lam = ld
            shift = 1
            while shift < c:
                lam = lam + jnp.where(trow >= shift, pltpu.roll(lam, shift, axis=0), 0.0)
                shift *= 2
            g_inv = jnp.exp(-lam)
            parts.append((bd(-kk * jnp.exp(lam - ld)), bd(r * jnp.exp(lam)), bd(kk * asig * g_inv),
                          bd(k * g_inv), bd(v), jnp.exp(lam[c - 1:c])))
        return tuple(cat([pt[i] for pt in parts]) for i in range(5)) + ([pt[5] for pt in parts],)

    a_bd, r_bd, b_bd, k_bd, v_bd, g_end = zip(*[prep(gi) for gi in grp])
    p = [_dot(jnp.concatenate([a_bd[gi], r_bd[gi]], axis=0),
              jnp.concatenate([b_bd[gi], k_bd[gi]], axis=0), _NT) for gi in grp]
    l_pow = [jnp.where(strict, p[gi][:rows, :rows], 0.0) for gi in grp]
    t_inv = [eye + l_pow[gi] for gi in grp]
    for _ in range(c.bit_length() - 2):
        l_pow = [_dot(l_pow[gi], l_pow[gi]) for gi in grp]
        t_inv = [t_inv[gi] + _dot(t_inv[gi], l_pow[gi]) for gi in grp]
    s = [[s_sc[gi * nb + bi] for bi in bat] for gi in grp]
    xs = [[_dot(jnp.concatenate([rsl(a_bd[gi], bi), rsl(r_bd[gi], bi)], axis=0), s[gi][bi], _NT)
           for bi in bat] for gi in grp]
    xs_a = [cat([xs[gi][bi][:rb] for bi in bat]) for gi in grp]
    xs_r = [cat([xs[gi][bi][rb:] for bi in bat]) for gi in grp]
    lkv = [_dot(jnp.where(strict, p[gi][:rows, rows:], 0.0), v_bd[gi]) for gi in grp]
    pkv = [_dot(jnp.where(incl, p[gi][rows:, rows:], 0.0), v_bd[gi]) for gi in grp]
    u = [_dot(t_inv[gi], xs_a[gi] + lkv[gi]) for gi in grp]
    y_bd = [xs_r[gi] + pkv[gi] + _dot(jnp.where(incl, p[gi][rows:, :rows], 0.0), u[gi]) for gi in grp]
    s_new = [[s[gi][bi] * g_end[gi][bi]
              + _dot(jnp.concatenate([rsl(u[gi], bi), rsl(v_bd[gi], bi)], axis=0),
                     jnp.concatenate([rsl(b_bd[gi], bi), rsl(k_bd[gi], bi)], axis=0) * g_end[gi][bi], _TN)
              for bi in bat] for gi in grp]
    for gi in grp:
        for bi in bat:
            y = y_bd[gi][bi * rb:bi * rb + c]
            for h in range(1, hb):
                y = y + y_bd[gi][bi * rb + h * c:bi * rb + (h + 1) * c]
            y_ref[bi, :, lanes[gi]] = y
            s_sc[gi * nb + bi] = s_new[gi][bi]

    @pl.when(last)
    def _():
        for gi in grp:
            for bi in bat:
                for h in range(hb):
                    sout_ref[bi, gi * hb + h] = s_new[gi][bi][h * n:(h + 1) * n, h * n:(h + 1) * n]


def wkv_chunked(r, ld, k, v, kk, asig, s0, n_heads, s_all, layer):
    bsz, t, d = r.shape
    n = d // n_heads
    hb = _pick(n_heads, (4, 2, 1))
    ng = _pick(n_heads // hb, (4, 2, 1))
    c = _pick(t, (WKV_CHUNK, 32, 16, 8))
    nb = _pick(bsz, tuple(x for x in (8, 4, 2, 1) if x * hb * c <= MXU_WIDTH))
    seq_spec = pl.BlockSpec((nb, c, ng * hb * n), lambda b, q, ch: (b, ch, q))
    st_spec = pl.BlockSpec((nb, ng * hb, n, n), lambda b, q, ch: (b, q, 0, 0))
    st_out_spec = pl.BlockSpec((None, nb, ng * hb, n, n), lambda b, q, ch: (layer, b, q, 0, 0))
    return pl.pallas_call(
        functools.partial(_wkv_chunk_kernel, c=c, hb=hb, n=n, ng=ng, nb=nb),
        grid=(bsz // nb, n_heads // (hb * ng), t // c),
        in_specs=[seq_spec] * 6 + [st_spec, pl.BlockSpec(memory_space=pl.ANY)],
        out_specs=[seq_spec, st_out_spec],
        out_shape=[jax.ShapeDtypeStruct((bsz, t, d), F32),
                   jax.ShapeDtypeStruct(s_all.shape, F32)],
        scratch_shapes=[pltpu.VMEM((ng * nb, hb * n, hb * n), F32)],
        input_output_aliases={7: 1},
        compiler_params=_cparams(("parallel", "parallel", "arbitrary")),
        name="wkv_chunked",
    )(r, ld, k, v, kk, asig, s0, s_all)


def _head_sum(x, n):
    wd = min(MXU_WIDTH, x.shape[1])
    gi = lax.broadcasted_iota(jnp.int32, (wd, wd), 0) // n
    gj = lax.broadcasted_iota(jnp.int32, (wd, wd), 1) // n
    g = jnp.where(gi == gj, 1.0, 0.0).astype(BF16)
    hi = x.astype(BF16)
    lo = (x - hi.astype(F32)).astype(BF16)
    parts = []
    for s in range(0, x.shape[1], wd):
        parts.append(jnp.dot(hi[:, s:s + wd], g, preferred_element_type=F32)
                     + jnp.dot(lo[:, s:s + wd], g, preferred_element_type=F32))
    return parts[0] if len(parts) == 1 else jnp.concatenate(parts, axis=1)


def _softplus(z):
    return jnp.maximum(z, 0.0) + jnp.log(1.0 + jnp.exp(-jnp.abs(z)))


def _rwkv_pre_kernel(*refs, n, has_v_lora):
    if has_v_lora:
        (k_ref, v_ref, hw_ref, ha_ref, w2_ref, a2_ref, w0_ref, a0_ref, kk_ref, ka_ref,
         vf_ref, hv_ref, v2_ref, v0_ref, ld_out, a_out, kk_out, k_out, v_out) = refs
    else:
        (k_ref, v_ref, hw_ref, ha_ref, w2_ref, a2_ref, w0_ref, a0_ref, kk_ref, ka_ref,
         ld_out, a_out, kk_out, k_out) = refs
    k = k_ref[...]
    w_pre = w0_ref[...] + jnp.dot(jnp.tanh(hw_ref[...]).astype(BF16), w2_ref[...], preferred_element_type=F32)
    ld_out[...] = -jnp.exp(-_softplus(-w_pre) - 0.5)
    a = jax.nn.sigmoid(a0_ref[...] + jnp.dot(ha_ref[...].astype(BF16), a2_ref[...], preferred_element_type=F32))
    a_out[...] = a
    kk = k * kk_ref[...]
    kk_out[...] = kk / jnp.maximum(jnp.sqrt(_head_sum(kk * kk, n)), 1e-12)
    k_out[...] = k * (1.0 + (a - 1.0) * ka_ref[...])
    if has_v_lora:
        v = v_ref[...]
        gate = jax.nn.sigmoid(v0_ref[...] + jnp.dot(hv_ref[...].astype(BF16), v2_ref[...],
                                                    preferred_element_type=F32))
        v_out[...] = v + (vf_ref[...] - v) * gate


def rwkv_pre(k, v, hw, ha, p, v_first, hv, n):
    m, d = k.shape
    bm = _pick(m, (256, 128, 64, 32, 16, 8))
    has_v_lora = p["v_lora"] is not None
    row = lambda width: pl.BlockSpec((bm, width), lambda i: (i, 0))
    full = lambda a: pl.BlockSpec(a.shape, lambda i: (0, 0))
    vec = lambda z: z.reshape(1, d)
    args = [k, v, hw, ha, p["w2"], p["a2"], vec(p["w0"]), vec(p["a0"]), vec(p["k_k"]), vec(p["k_a"])]
    specs = [row(d), row(d), row(hw.shape[1]), row(ha.shape[1])] + [full(a) for a in args[4:]]
    n_out = 4
    if has_v_lora:
        v0, _, v2 = p["v_lora"]
        extra = [v_first, hv, v2, vec(v0)]
        args += extra
        specs += [row(d), row(hv.shape[1]), full(v2), full(extra[3])]
        n_out = 5
    outs = pl.pallas_call(
        functools.partial(_rwkv_pre_kernel, n=n, has_v_lora=has_v_lora),
        grid=(m // bm,),
        in_specs=specs,
        out_specs=[row(d)] * n_out,
        out_shape=[jax.ShapeDtypeStruct((m, d), F32)] * n_out,
        compiler_params=_cparams(("parallel",)),
        name="rwkv_pre",
    )(*args)
    return tuple(outs) + ((v,) if not has_v_lora else ())


def _rwkv_post_kernel(y_ref, r_ref, k_ref, v_ref, hg_ref, g2_ref, lnw_ref, lnb_ref, rk_ref, o_ref, *, n):
    y = y_ref[...]
    mu = _head_sum(y, n) / n
    dlt = y - mu
    var = _head_sum(dlt * dlt, n) / n
    yn = dlt * lax.rsqrt(var + n * GN_EPS_PER_CHANNEL) * lnw_ref[...] + lnb_ref[...]
    bonus = _head_sum(r_ref[...] * k_ref[...] * rk_ref[...], n) * v_ref[...]
    g = jnp.dot(jax.nn.sigmoid(hg_ref[...]).astype(BF16), g2_ref[...], preferred_element_type=F32)
    o_ref[...] = ((yn + bonus) * g).astype(o_ref.dtype)


def rwkv_post(y, r, k, v, hg, p, n):
    m, d = y.shape
    bm = _pick(m, (256, 128, 64, 32, 16, 8))
    row = lambda width: pl.BlockSpec((bm, width), lambda i: (i, 0))
    full = lambda a: pl.BlockSpec(a.shape, lambda i: (0, 0))
    vec = lambda z: z.reshape(1, d)
    consts = [p["g2"], vec(p["ln_w"]), vec(p["ln_b"]), vec(p["r_k"])]
    return pl.pallas_call(
        functools.partial(_rwkv_post_kernel, n=n),
        grid=(m // bm,),
        in_specs=[row(d)] * 4 + [row(hg.shape[1])] + [full(a) for a in consts],
        out_specs=row(d),
        out_shape=jax.ShapeDtypeStruct((m, d), BF16),
        compiler_params=_cparams(("parallel",)),
        name="rwkv_post",
    )(y, r, k, v, hg, *consts)


def _flash_kernel(qn_ref, qr_ref, kn_ref, kr_ref, v_ref, o_ref, *, bq, n_q, hp, dh, dv, scale):
    qi = pl.program_id(2)
    dn = (((1,), (1,)), ((), ()))
    heads = range(hp)
    hs = [slice(h * dh, (h + 1) * dh) for h in heads]
    vs = [slice(h * dv, (h + 1) * dv) for h in heads]

    for i in range(n_q):
        @pl.when(qi == i)
        def _(i=i):
            kv_len = (i + 1) * bq
            past = kv_len - bq
            tri = (lax.broadcasted_iota(jnp.int32, (bq, bq), 1) <= lax.broadcasted_iota(jnp.int32, (bq, bq), 0))
            s = [(lax.dot_general(qn_ref[0, :, hs[h]], kn_ref[0, :kv_len, hs[h]], dn, preferred_element_type=F32)
                  + lax.dot_general(qr_ref[0, :, hs[h]], kr_ref[0, :kv_len, hs[h]], dn,
                                    preferred_element_type=F32)) * scale for h in heads]
            s_diag = [jnp.where(tri, s[h][:, past:], -jnp.inf) for h in heads]
            m = [jnp.max(s_diag[h], axis=-1, keepdims=True) for h in heads]
            if past:
                m = [jnp.maximum(m[h], jnp.max(s[h][:, :past], axis=-1, keepdims=True)) for h in heads]
            p_diag = [jnp.exp(s_diag[h] - m[h]) for h in heads]
            l = [jnp.sum(p_diag[h], axis=-1, keepdims=True) for h in heads]
            o = [jnp.dot(p_diag[h].astype(BF16), v_ref[0, past:kv_len, vs[h]], preferred_element_type=F32)
                 for h in heads]
            if past:
                p_past = [jnp.exp(s[h][:, :past] - m[h]) for h in heads]
                l = [l[h] + jnp.sum(p_past[h], axis=-1, keepdims=True) for h in heads]
                o = [o[h] + jnp.dot(p_past[h].astype(BF16), v_ref[0, :past, vs[h]], preferred_element_type=F32)
                     for h in heads]
            for h in heads:
                o_ref[0, :, vs[h]] = (o[h] / l[h]).astype(o_ref.dtype)


def flash_prompt(qn, qr, kn, kr, v, n_heads, scale):
    bsz, s_len, _ = qn.shape
    dh = qn.shape[-1] // n_heads
    dv = v.shape[-1] // n_heads
    bq = _pick(s_len, (512, 256, 128, 64, 32, 16, 8))
    n_q = s_len // bq
    hp = _pick(n_heads, (2, 1))
    q_spec = pl.BlockSpec((1, bq, hp * dh), lambda b, h, i: (b, i, h))
    k_spec = pl.BlockSpec((1, s_len, hp * dh), lambda b, h, i: (b, 0, h))
    return pl.pallas_call(
        functools.partial(_flash_kernel, bq=bq, n_q=n_q, hp=hp, dh=dh, dv=dv, scale=scale),
        grid=(bsz, n_heads // hp, n_q),
        in_specs=[q_spec, q_spec, k_spec, k_spec,
                  pl.BlockSpec((1, s_len, hp * dv), lambda b, h, i: (b, 0, h))],
        out_specs=pl.BlockSpec((1, bq, hp * dv), lambda b, h, i: (b, i, h)),
        out_shape=jax.ShapeDtypeStruct((bsz, s_len, n_heads * dv), BF16),
        compiler_params=_cparams(("parallel", "parallel", "arbitrary")),
        name="flash_prompt",
    )(qn, qr, kn, kr, v)


def _mla_head_kernel(*refs, n_heads, d_qk, half, is_query):
    tile = lambda z: jnp.concatenate([z] * n_heads, axis=1)
    if is_query:
        xn_ref, xr_ref, cos_ref, sin_a_ref, sin_b_ref, gn_ref, gr_ref, on_ref, or_ref = refs
        xr = xr_ref[...]
        width = xr.shape[1]
        xr = (xr * tile(cos_ref[...]) + pltpu.roll(xr, width - half, axis=1) * tile(sin_a_ref[...])
              + pltpu.roll(xr, half, axis=1) * tile(sin_b_ref[...]))
    else:
        xn_ref, kpe_ref, gn_ref, gr_ref, on_ref, or_ref = refs
        kpe = kpe_ref[...]
        pad = jnp.zeros((kpe.shape[0], xn_ref.shape[1] // n_heads - kpe.shape[1]), F32)
        xr = tile(jnp.concatenate([kpe, pad], axis=1))
    xn = xn_ref[...]
    inv = lax.rsqrt(_head_sum(xn * xn + xr * xr, xn.shape[1] // n_heads) / d_qk + RMS_EPS)
    on_ref[...] = (xn * inv * gn_ref[...]).astype(on_ref.dtype)
    or_ref[...] = (xr * inv * gr_ref[...]).astype(or_ref.dtype)


def mla_head_norm(x, gain_n, gain_r, n_heads, d_qk, d_rope, rope_tabs=None, k_pe=None):
    m = x.shape[0]
    is_query = rope_tabs is not None
    wd = gain_n.shape[1]
    dh = wd // n_heads
    bm = _pick(m, (256, 128, 64, 32, 16, 8))
    row = lambda width, blk=0: pl.BlockSpec((bm, width), lambda i, blk=blk: (i, blk))
    full = lambda a: pl.BlockSpec(a.shape, lambda i: (0, 0))
    if is_query:
        args = [x, x, *rope_tabs, gain_n, gain_r]
        specs = [row(wd, 0), row(wd, 1), row(dh), row(dh), row(dh), full(gain_n), full(gain_r)]
    else:
        args = [x, k_pe, gain_n, gain_r]
        specs = [row(wd), row(k_pe.shape[1]), full(gain_n), full(gain_r)]
    return pl.pallas_call(
        functools.partial(_mla_head_kernel, n_heads=n_heads, d_qk=d_qk, half=d_rope // 2, is_query=is_query),
        grid=(m // bm,),
        in_specs=specs,
        out_specs=[row(wd), row(wd)],
        out_shape=[jax.ShapeDtypeStruct((m, wd), BF16)] * 2,
        compiler_params=_cparams(("parallel",)),
        name="mla_head_norm",
    )(*args)


def _kscale_kernel(pt_ref, *refs, npg, n_heads, d_qk):
    del pt_ref
    c_refs = refs[:npg]
    kpe_refs = refs[npg:2 * npg]
    wt_ref = refs[2 * npg]
    s_ref, c_out, kpe_out = refs[2 * npg + 1:]
    page = c_refs[0].shape[1]
    pair = 2 if npg % 2 == 0 else 1
    for p in range(0, npg, pair):
        keys = slice(p * page, (p + pair) * page)
        c = jnp.concatenate([c_refs[p + q][0] for q in range(pair)], axis=0).astype(BF16)
        kpe_t = jnp.concatenate([kpe_refs[p + q][0] for q in range(pair)], axis=1)
        k_t = lax.dot_general(wt_ref[...], c, (((1,), (1,)), ((), ())), preferred_element_type=F32)
        sq = k_t * k_t
        ss = jnp.sum(sq.reshape(n_heads, sq.shape[0] // n_heads, sq.shape[1]), axis=1)
        pe = jnp.sum(kpe_t * kpe_t, axis=0, keepdims=True)
        s_ref[0, :, keys] = lax.rsqrt((ss + pe) / d_qk + RMS_EPS)
        c_out[0, keys, :] = c
        kpe_out[0, :, keys] = kpe_t.astype(BF16)


def paged_key_scale(cache_c, cache_kpe_t, page_table_flat, w_uk_t, n_heads, d_qk, npg):
    n_used = page_table_flat.shape[0]
    page, d_c = cache_c.shape[1:]
    d_r = cache_kpe_t.shape[1]
    n_grp = n_used // npg
    nk = npg * page

    def page_spec(shape, p):
        return pl.BlockSpec((1,) + shape, lambda g, pt, p=p: (pt[g * npg + p], 0, 0))

    in_specs = ([page_spec((page, d_c), p) for p in range(npg)] + [page_spec((d_r, page), p) for p in range(npg)]
                + [pl.BlockSpec(w_uk_t.shape, lambda g, pt: (0, 0))])
    out = lambda shape: pl.BlockSpec((1,) + shape, lambda g, pt: (g, 0, 0))
    grid_spec = pltpu.PrefetchScalarGridSpec(
        num_scalar_prefetch=1, grid=(n_grp,), in_specs=in_specs,
        out_specs=[out((n_heads, nk)), out((nk, d_c)), out((d_r, nk))])
    return pl.pallas_call(
        functools.partial(_kscale_kernel, npg=npg, n_heads=n_heads, d_qk=d_qk),
        grid_spec=grid_spec,
        out_shape=[jax.ShapeDtypeStruct((n_grp, n_heads, nk), F32),
                   jax.ShapeDtypeStruct((n_grp, nk, d_c), BF16),
                   jax.ShapeDtypeStruct((n_grp, d_r, nk), BF16)],
        compiler_params=_cparams(("arbitrary",)),
        name="paged_key_scale",
    )(page_table_flat, *([cache_c] * npg), *([cache_kpe_t] * npg), w_uk_t)


def _paged_attn_kernel(ql_ref, qr_ref, cnew_ref, pnew_ref, snew_ref, c_ref, p_ref, s_ref, o_ref,
                       m_sc, l_sc, acc_sc, *, bb, n_heads, t_len, scale):
    g = pl.program_id(1)
    rows = n_heads * t_len
    dn = (((1,), (1,)), ((), ()))

    def process(c, kpe_t, key_scale, mask):
        bs = range(bb)
        sc = [lax.dot_general(ql_ref[b], c[b], dn, preferred_element_type=F32)
              + jnp.dot(qr_ref[b], kpe_t[b], preferred_element_type=F32) for b in bs]
        sc = [sc[b] * (jnp.concatenate([key_scale[b]] * t_len, axis=0) * scale) for b in bs]
        if mask is not None:
            sc = [jnp.where(mask, sc[b], -jnp.inf) for b in bs]
        m_new = [jnp.maximum(m_sc[b], jnp.max(sc[b], axis=-1, keepdims=True)) for b in bs]
        alpha = [jnp.exp(m_sc[b] - m_new[b]) for b in bs]
        p = [jnp.exp(sc[b] - m_new[b]) for b in bs]
        pv = [jnp.dot(p[b].astype(BF16), c[b], preferred_element_type=F32) for b in bs]
        for b in bs:
            l_sc[b] = l_sc[b] * alpha[b] + jnp.sum(p[b], axis=-1, keepdims=True)
            acc_sc[b] = acc_sc[b] * alpha[b] + pv[b]
            m_sc[b] = m_new[b]

    @pl.when(g == 0)
    def _():
        m_sc[...] = jnp.full(m_sc.shape, -jnp.inf, F32)
        l_sc[...] = jnp.zeros(l_sc.shape, F32)
        acc_sc[...] = jnp.zeros(acc_sc.shape, F32)
        nk = cnew_ref.shape[1]
        t_row = lax.broadcasted_iota(jnp.int32, (rows, nk), 0) // n_heads
        key = lax.broadcasted_iota(jnp.int32, (rows, nk), 1)
        process([cnew_ref[b] for b in range(bb)], [pnew_ref[b] for b in range(bb)],
                [snew_ref[b] for b in range(bb)], key <= t_row)

    process([c_ref[b, 0] for b in range(bb)], [p_ref[b, 0] for b in range(bb)],
            [s_ref[b, 0] for b in range(bb)], None)

    @pl.when(g == pl.num_programs(1) - 1)
    def _():
        for b in range(bb):
            o_ref[b] = acc_sc[b] / l_sc[b]


def paged_attention(q_lat, q_r, c_new, kpe_new_t, s_new, c_past, kpe_past_t, s_past, n_heads, t_len, scale):
    bsz, rows, d_c = q_lat.shape
    d_r = q_r.shape[2]
    _, n_grp, nk, _ = c_past.shape
    kn = c_new.shape[1]
    bb = _pick(bsz, (2, 1))

    def per_b(shape):
        return pl.BlockSpec((bb,) + shape, lambda b, g: (b, 0, 0))

    def per_g(shape):
        return pl.BlockSpec((bb, 1) + shape, lambda b, g: (b, g, 0, 0))

    return pl.pallas_call(
        functools.partial(_paged_attn_kernel, bb=bb, n_heads=n_heads, t_len=t_len, scale=scale),
        grid=(bsz // bb, n_grp),
        in_specs=[per_b((rows, d_c)), per_b((rows, d_r)), per_b((kn, d_c)), per_b((d_r, kn)),
                  per_b((n_heads, kn)), per_g((nk, d_c)), per_g((d_r, nk)), per_g((n_heads, nk))],
        out_specs=per_b((rows, d_c)),
        out_shape=jax.ShapeDtypeStruct((bsz, rows, d_c), F32),
        scratch_shapes=[pltpu.VMEM((bb, rows, 1), F32), pltpu.VMEM((bb, rows, 1), F32),
                        pltpu.VMEM((bb, rows, d_c), F32)],
        compiler_params=_cparams(("parallel", "arbitrary")),
        name="paged_attention",
    )(q_lat, q_r, c_new, kpe_new_t, s_new, c_past, kpe_past_t, s_past)


def _rms_norm(x, g):
    return x * lax.rsqrt(jnp.mean(x * x, axis=-1, keepdims=True) + RMS_EPS) * g


def _rope(x, pos):
    half = x.shape[-1] // 2
    inv = ROPE_BASE ** (-jnp.arange(half, dtype=F32) / half)
    ang = pos.astype(F32)[:, None] * inv[None, :]
    cos = jnp.cos(ang)[None, :, None, :]
    sin = jnp.sin(ang)[None, :, None, :]
    x1, x2 = x[..., :half], x[..., half:]
    return jnp.concatenate([x1 * cos - x2 * sin, x1 * sin + x2 * cos], axis=-1)


def _conv_ffn(x, prev, p):
    hid, state = ffn_in(x, p["norm"], prev, p["w_gate"], p["w_up"], p["conv_w"], p["conv_b"])
    return mm(hid, p["w_down"], res=x.reshape(hid.shape[0], -1)).reshape(x.shape), state


def _rwkv7_time_mix(x, shift_prev, wkv_prev, v_first, p, wkv_all, layer):
    bsz, t, d = x.shape
    n_heads, n = p["r_k"].shape
    h = _rms_norm(x, p["norm"])
    prev = jnp.concatenate([shift_prev[:, None, :], h[:, :-1]], axis=1)
    xx = prev - h
    xr, xw, xk, xv, xa, xg = ((h + xx * p["mix"][i]).astype(BF16) for i in range(6))
    flat = lambda z: z.reshape(bsz * t, -1)
    r = mm(flat(xr), p["wr"])
    k = mm(flat(xk), p["wk"])
    v = mm(flat(xv), p["wv"])
    hv = None if p["v_lora"] is None else mm(flat(xv), p["v_lora"][1])
    log_decay, a, kk, k, v = rwkv_pre(k, v, mm(flat(xw), p["w1"]), mm(flat(xa), p["a1"]), p, v_first, hv, n)
    if p["v_lora"] is None:
        v_first = v
    seq = lambda z: z.reshape(bsz, t, d)
    y, s = wkv_chunked(seq(r), seq(log_decay), seq(k), seq(v), seq(kk), seq(a), wkv_prev, n_heads,
                       wkv_all, layer)
    out = rwkv_post(flat(y), r, k, v, mm(flat(xg), p["g1"]), p, n)
    return mm(out, p["wo"], res=flat(x)).reshape(x.shape), h[:, -1], s, v_first


def _mla_kv_side(x, pos, kv):
    d_c = kv["norm_c"].shape[0]
    ckr = mm(x, kv["w_dkv_kr"], norm_gain=kv["norm"])
    c = _rms_norm(ckr[..., :d_c], kv["norm_c"])
    k_pe = _rope(ckr[..., d_c:][:, :, None, :], pos)[:, :, 0, :]
    return c, k_pe


def _head_gains(g, n_heads, d_nope, other=None):
    if other is not None:
        g = g * other
    g_r = jnp.pad(g[d_nope:], (0, 2 * d_nope - g.shape[0]))
    return jnp.tile(g[:d_nope], n_heads)[None], jnp.tile(g_r, n_heads)[None]


def _mla_query(x, pos, p, kv, n_heads, d_nope, d_rope, absorb_key_gain):
    bsz, t, _ = x.shape
    cq = mm(x.reshape(bsz * t, -1), p["w_dq"], norm_gain=p["norm"])
    q = mm(cq, p["w_uq"], norm_gain=p["norm_cq"])
    half = d_rope // 2
    inv = ROPE_BASE ** (-jnp.arange(half, dtype=F32) / half)
    ang = pos.astype(F32)[:, None] * inv[None, :]
    cos, sin, zero = jnp.cos(ang), jnp.sin(ang), jnp.zeros((t, half), F32)
    lane_pad = jnp.zeros((t, d_nope - d_rope), F32)
    tabs = tuple(jnp.tile(jnp.concatenate(parts + [lane_pad], axis=1), (bsz, 1))
                 for parts in ([cos, cos], [-sin, zero], [zero, sin]))
    g_n, g_r = _head_gains(p["norm_q"], n_heads, d_nope, kv["norm_k"] if absorb_key_gain else None)
    return mla_head_norm(q, g_n, g_r, n_heads, d_nope + d_rope, d_rope, rope_tabs=tabs)


def _prompt_attend_fn(c, k_pe, kv, n_heads, d_nope):
    bsz, s_len, _ = c.shape
    d_rope = k_pe.shape[-1]
    d_qk = d_nope + d_rope
    g_n, g_r = _head_gains(kv["norm_k"], n_heads, d_nope)
    k_n, k_r = mla_head_norm(mm(c.reshape(bsz * s_len, -1), kv["w_uk_2d"]), g_n, g_r, n_heads, d_qk, d_rope,
                             k_pe=k_pe.reshape(bsz * s_len, d_rope))
    v = mm(c, kv["w_uv_2d"], BF16)
    seq = lambda z: z.reshape(bsz, s_len, -1)

    def attend(q):
        return flash_prompt(seq(q[0]), seq(q[1]), seq(k_n), seq(k_r), v, n_heads, d_qk ** -0.5)

    return attend


def _sample_attend_fn(c_new, kpe_new, cache_c, cache_kpe, page_table, kv, n_heads, d_nope):
    bsz, t, d_c = c_new.shape
    d_r = kpe_new.shape[-1]
    d_qk = d_nope + d_r
    page = cache_c.shape[1]
    pt_flat = page_table.reshape(-1)
    npg = _pick(page_table.shape[1], (PAGES_PER_STEP, 8, 4, 2, 1))

    k_nope_new = mm(c_new, kv["w_uk_2d"]).reshape(bsz, t, n_heads, d_nope)
    ss = jnp.sum(k_nope_new * k_nope_new, axis=-1) + jnp.sum(kpe_new * kpe_new, axis=-1)[..., None]
    s_new = lax.rsqrt(ss / d_qk + RMS_EPS)
    kn = page
    c_new_p = jnp.pad(c_new.astype(BF16), ((0, 0), (0, kn - t), (0, 0)))
    kpe_new_t = jnp.pad(kpe_new.astype(BF16).transpose(0, 2, 1), ((0, 0), (0, 0), (0, kn - t)))
    s_new_p = jnp.pad(s_new.transpose(0, 2, 1), ((0, 0), (0, 0), (0, kn - t)), constant_values=1.0)

    s_past, c_past, kpe_past_t = paged_key_scale(cache_c, cache_kpe.transpose(0, 2, 1), pt_flat, kv["w_uk_t"],
                                                 n_heads, d_qk, npg)
    n_grp = page_table.shape[1] // npg
    nk = npg * page
    s_past = s_past.reshape(bsz, n_grp, n_heads, nk)
    c_past = c_past.reshape(bsz, n_grp, nk, d_c)
    kpe_past_t = kpe_past_t.reshape(bsz, n_grp, d_r, nk)
    scale = d_qk ** -0.5

    def attend(q):
        q_lat = hmm(q[0], kv["w_uk_hdc"], BF16).reshape(bsz, t * n_heads, d_c)
        q_r = q[1].reshape(bsz, t * n_heads, d_nope)[..., :d_r]
        o_lat = paged_attention(q_lat, q_r, c_new_p, kpe_new_t, s_new_p, c_past, kpe_past_t, s_past,
                                n_heads, t, scale)
        o = hmm(o_lat.reshape(bsz * t, n_heads * d_c), kv["w_uv_hcd"])
        return o.reshape(bsz, t, -1)

    return attend


def kernel(x_prompt, x_sample, cache_kv_latent, cache_k_pe, state_wkv, state_shift, state_conv, page_table,
           ffn_norm, ffn_w_gate, ffn_w_up, ffn_conv_w, ffn_conv_b, ffn_w_down,
           rw_norm, rw_mix, rw_w0, rw_w1, rw_w2, rw_a0, rw_a1, rw_a2, rw_v0, rw_v1, rw_v2,
           rw_g1, rw_g2, rw_k_k, rw_k_a, rw_r_k, rw_wr, rw_wk, rw_wv, rw_ln_w, rw_ln_b, rw_wo,
           kv_norm, kv_w_dkv, kv_norm_c, kv_w_kr, kv_w_uk, kv_w_uv, kv_norm_k,
           mla_norm, mla_w_dq, mla_norm_cq, mla_w_uq, mla_norm_q, mla_wo):
    depth = ffn_norm.shape[0]
    n_a = rw_norm.shape[0]
    d_c, n_heads, d_nope = kv_w_uk.shape
    d_v = kv_w_uv.shape[2]
    bf = lambda z: z.astype(BF16)

    ffn = [dict(norm=ffn_norm[l], w_gate=bf(ffn_w_gate[l]), w_up=bf(ffn_w_up[l]), conv_w=ffn_conv_w[l],
                conv_b=ffn_conv_b[l], w_down=bf(ffn_w_down[l])) for l in range(depth)]
    rw = [dict(norm=rw_norm[l], mix=rw_mix[l], w0=rw_w0[l], w1=bf(rw_w1[l]), w2=bf(rw_w2[l]),
               a0=rw_a0[l], a1=bf(rw_a1[l]), a2=bf(rw_a2[l]), g1=bf(rw_g1[l]), g2=bf(rw_g2[l]),
               k_k=rw_k_k[l], k_a=rw_k_a[l], r_k=rw_r_k[l], wr=bf(rw_wr[l]), wk=bf(rw_wk[l]),
               wv=bf(rw_wv[l]), ln_w=rw_ln_w[l], ln_b=rw_ln_b[l], wo=bf(rw_wo[l]),
               v_lora=None if l == 0 else (rw_v0[l - 1], bf(rw_v1[l - 1]), bf(rw_v2[l - 1])))
          for l in range(n_a)]
    kv = dict(norm=kv_norm, norm_c=kv_norm_c, norm_k=kv_norm_k,
              w_dkv_kr=bf(jnp.concatenate([kv_w_dkv, kv_w_kr], axis=1)),
              w_uk_2d=bf(kv_w_uk.reshape(d_c, n_heads * d_nope)),
              w_uv_2d=bf(kv_w_uv.reshape(d_c, n_heads * d_v)),
              w_uk_t=bf(kv_w_uk.reshape(d_c, n_heads * d_nope).T),
              w_uk_hdc=bf(kv_w_uk.transpose(1, 2, 0)),
              w_uv_hcd=bf(kv_w_uv.transpose(1, 0, 2)))
    d_qc = mla_w_uq.shape[1]
    d_rope = kv_w_kr.shape[1]

    def split_heads(w_uq):
        w_r = jnp.pad(w_uq[:, :, d_nope:], ((0, 0), (0, 0), (0, d_nope - d_rope)))
        return jnp.concatenate([w_uq[:, :, :d_nope].reshape(d_qc, -1), w_r.reshape(d_qc, -1)], axis=1)

    mla = [dict(norm=mla_norm[b], w_dq=bf(mla_w_dq[b]), norm_cq=mla_norm_cq[b],
                w_uq=bf(split_heads(mla_w_uq[b])), norm_q=mla_norm_q[b],
                wo=bf(mla_wo[b])) for b in range(depth - n_a)]

    def trunk(x, pos, shift_in, wkv_in, conv_in, make_attend, absorb_key_gain):
        v_first = attend = c = k_pe = None
        shifts, convs = [], []
        wkvs = jnp.zeros(wkv_in.shape, F32)
        for layer in range(depth):
            if layer < n_a:
                x, sh, wkvs, v_first = _rwkv7_time_mix(x, shift_in[layer], wkv_in[layer], v_first, rw[layer],
                                                       wkvs, layer)
                shifts.append(sh)
            else:
                if layer == n_a:
                    c, k_pe = _mla_kv_side(x, pos, kv)
                    attend = make_attend(c, k_pe)
                p = mla[layer - n_a]
                q = _mla_query(x, pos, p, kv, n_heads, d_nope, d_rope, absorb_key_gain)
                x = mm(attend(q), p["wo"], res=x)
            x, cv = _conv_ffn(x, conv_in[layer], ffn[layer])
            convs.append(cv)
        return x, c, k_pe, wkvs, jnp.stack(shifts), jnp.stack(convs)

    bp, sp, d = x_prompt.shape
    n_rw_heads, n = rw_r_k.shape[1:]
    d_ff = ffn_w_gate.shape[2]
    out_p = trunk(
        x_prompt, jnp.arange(sp),
        jnp.zeros((n_a, bp, d), F32), jnp.zeros((n_a, bp, n_rw_heads, n, n), F32),
        jnp.zeros((depth, bp, CONV_W - 1, d_ff), F32),
        lambda c, kp: _prompt_attend_fn(c, kp, kv, n_heads, d_nope), False)

    past_len = page_table.shape[1] * cache_kv_latent.shape[1]
    out_s = trunk(
        x_sample, past_len + jnp.arange(x_sample.shape[1]), state_shift, state_wkv, state_conv,
        lambda c, kp: _sample_attend_fn(c, kp, cache_kv_latent, cache_k_pe, page_table, kv, n_heads, d_nope),
        True)

    return (out_p[0], out_s[0]) + tuple(out_p[1:]) + tuple(out_s[1:])
```

```python
import functools

import jax
import jax.numpy as jnp
from jax import lax
from jax.experimental import pallas as pl
from jax.experimental.pallas import tpu as pltpu

F32 = jnp.float32
BF16 = jnp.bfloat16

RMS_EPS = 1e-6
ROPE_BASE = 10000.0
GN_EPS_PER_CHANNEL = 1e-5
CONV_W = 3

VMEM_LIMIT_BYTES = 52 * 1024 * 1024
MM_VMEM_BUDGET_BYTES = 44 * 1024 * 1024
MXU_WIDTH = 256
PAGES_PER_STEP = 32
ROW_BLOCK = 1024
WKV_CHUNK = 64
_ROW_BLOCKS = (1024, 512, 256, 128, 64, 32, 16, 8)


def _pick(n, candidates):
    for c in candidates:
        if c <= n and n % c == 0:
            return c
    return n


def _cparams(sem):
    return pltpu.CompilerParams(dimension_semantics=sem, vmem_limit_bytes=VMEM_LIMIT_BYTES)


def _mm_kernel(*refs, has_norm, has_res):
    x_ref, w_ref = refs[:2]
    rest = list(refs[2:])
    x = x_ref[...]
    if has_norm:
        x = x * lax.rsqrt(jnp.mean(x * x, axis=-1, keepdims=True) + RMS_EPS) * rest.pop(0)[...]
    acc = jnp.dot(x.astype(BF16), w_ref[...], preferred_element_type=F32)
    if has_res:
        acc = rest.pop(0)[...] + acc
    o_ref = rest.pop(0)
    o_ref[...] = acc.astype(o_ref.dtype)


def _mm_col_block(bm, k, n, x_bytes, out_bytes):
    cands = [n] if n <= 1024 else [c for c in (1024, 512, 256, 128) if n % c == 0]
    for bn in cands:
        if 2 * (bm * k * x_bytes + k * bn * 2 + bm * bn * out_bytes) <= MM_VMEM_BUDGET_BYTES:
            return bn
    return cands[-1]


def _weight_block(w, bn):
    if isinstance(w, tuple):
        arr, layer = w
        return arr, pl.BlockSpec((None, arr.shape[1], bn), lambda i, j: (layer, 0, j))
    return w, pl.BlockSpec((w.shape[0], bn), lambda i, j: (0, j))


def _weight_shape(w):
    return (w[0] if isinstance(w, tuple) else w).shape[-2:]


def mm(x, w, out_dtype=F32, res=None, norm_gain=None):
    lead = x.shape[:-1]
    k, n = _weight_shape(w)
    x2 = x.reshape(-1, k)
    m = x2.shape[0]
    bm = _pick(m, _ROW_BLOCKS)
    out_bytes = jnp.dtype(out_dtype).itemsize * (1 if res is None else 2)
    bn = _mm_col_block(bm, k, n, x2.dtype.itemsize, out_bytes)
    out_spec = pl.BlockSpec((bm, bn), lambda i, j: (i, j))
    w_arr, w_spec = _weight_block(w, bn)
    args = [x2, w_arr]
    in_specs = [pl.BlockSpec((bm, k), lambda i, j: (i, 0)), w_spec]
    if norm_gain is not None:
        args.append(norm_gain.reshape(1, k))
        in_specs.append(pl.BlockSpec((1, k), lambda i, j: (0, 0)))
    if res is not None:
        args.append(res.reshape(m, n))
        in_specs.append(out_spec)
    out = pl.pallas_call(
        functools.partial(_mm_kernel, has_norm=norm_gain is not None, has_res=res is not None),
        grid=(m // bm, n // bn),
        in_specs=in_specs,
        out_specs=out_spec,
        out_shape=jax.ShapeDtypeStruct((m, n), out_dtype),
        compiler_params=_cparams(("parallel", "arbitrary")),
        name="mm",
    )(*args)
    return out.reshape(lead + (n,))


def _hmm_kernel(x_ref, w_ref, o_ref):
    o_ref[...] = jnp.dot(x_ref[...].astype(BF16), w_ref[0], preferred_element_type=F32).astype(o_ref.dtype)


def hmm(x, w, out_dtype=F32):
    m = x.shape[0]
    n_heads, k, n = w.shape
    bm = _pick(m, _ROW_BLOCKS)
    return pl.pallas_call(
        _hmm_kernel,
        grid=(n_heads, m // bm),
        in_specs=[pl.BlockSpec((bm, k), lambda h, i: (i, h)),
                  pl.BlockSpec((1, k, n), lambda h, i: (h, 0, 0))],
        out_specs=pl.BlockSpec((bm, n), lambda h, i: (i, h)),
        out_shape=jax.ShapeDtypeStruct((m, n_heads * n), out_dtype),
        compiler_params=_cparams(("parallel", "parallel")),
        name="hmm",
    )(x, w)


def _ffn_in_kernel(*refs, t_len, bm, seq_in_block):
    if seq_in_block:
        x_ref, g_ref, wg_ref, wu_ref, cw_ref, cb_ref, e_ref, hid_ref, tail_ref, h_sc = refs
    else:
        x_ref, xp_ref, g_ref, wg_ref, wu_ref, cw_ref, cb_ref, e_ref, hid_ref, tail_ref, h_sc = refs

    def norm(z):
        return (z * lax.rsqrt(jnp.mean(z * z, axis=-1, keepdims=True) + RMS_EPS) * g_ref[...]).astype(BF16)

    @pl.when(pl.program_id(1) == 0)
    def _():
        h_sc[...] = norm(x_ref[...])

    x = h_sc[...]
    wg = wg_ref[...]
    u = jnp.dot(x, wg, preferred_element_type=F32)
    up = jnp.dot(x, wu_ref[...], preferred_element_type=F32)
    row = lax.broadcasted_iota(jnp.int32, u.shape, 0)
    if seq_in_block:
        t = row % t_len
        u1 = jnp.where(t < 1, e_ref[0], pltpu.roll(u, 1, axis=0))
        u2 = jnp.where(t < 2, e_ref[1], pltpu.roll(u, 2, axis=0))
        tail_ref[...] = u
    else:
        prev8 = jnp.dot(norm(xp_ref[...]), wg, preferred_element_type=F32)
        seq_start = (pl.program_id(0) * bm) % t_len == 0
        prev8 = jnp.where(seq_start, e_ref[0], prev8)
        u1 = jnp.where(row < 1, prev8[7:8], pltpu.roll(u, 1, axis=0))
        u2 = jnp.where(row < 1, prev8[6:7], jnp.where(row < 2, prev8[7:8], pltpu.roll(u, 2, axis=0)))
        tail_ref[0] = u[bm - 8:bm]
    conv = cb_ref[...] + cw_ref[0:1] * u2 + cw_ref[1:2] * u1 + cw_ref[2:3] * u
    hid_ref[...] = (conv * jax.nn.sigmoid(conv) * up).astype(hid_ref.dtype)


def ffn_in(x, gain, prev, w_gate, w_up, conv_w, conv_b):
    bsz, t_len, d = x.shape
    f = _weight_shape(w_gate)[1]
    m = bsz * t_len
    x = x.reshape(m, d)
    bn = _pick(f, (512, 256, 128))
    seq_in_block = t_len < ROW_BLOCK
    if seq_in_block:
        bm = _pick(m, tuple(c for c in _ROW_BLOCKS if c <= ROW_BLOCK and c % t_len == 0))
        tpos = jnp.arange(t_len)[None, :, None]
        e = jnp.stack([jnp.broadcast_to(prev[:, 1:2], (bsz, t_len, f)),
                       jnp.where(tpos == 0, prev[:, 0:1], prev[:, 1:2])]).reshape(2, m, f)
        lead = []
        lead_specs = []
        e_spec = pl.BlockSpec((2, bm, bn), lambda i, j: (0, i, j))
        tail_shape = jax.ShapeDtypeStruct((m, f), F32)
        tail_spec = pl.BlockSpec((bm, bn), lambda i, j: (i, j))
    else:
        bm = _pick(t_len, tuple(c for c in _ROW_BLOCKS if c <= ROW_BLOCK))
        e = jnp.pad(prev, ((0, 0), (6, 0), (0, 0)))
        lead = [x]
        lead_specs = [pl.BlockSpec((8, d), lambda i, j: (jnp.maximum(i * (bm // 8) - 1, 0), 0))]
        e_spec = pl.BlockSpec((1, 8, bn), lambda i, j: ((i * bm) // t_len, 0, j))
        tail_shape = jax.ShapeDtypeStruct((m // bm, 8, f), F32)
        tail_spec = pl.BlockSpec((1, 8, bn), lambda i, j: (i, 0, j))
    w_gate, wg_spec = _weight_block(w_gate, bn)
    w_up, wu_spec = _weight_block(w_up, bn)
    hid, tail = pl.pallas_call(
        functools.partial(_ffn_in_kernel, t_len=t_len, bm=bm, seq_in_block=seq_in_block),
        grid=(m // bm, f // bn),
        in_specs=([pl.BlockSpec((bm, d), lambda i, j: (i, 0))] + lead_specs
                  + [pl.BlockSpec((1, d), lambda i, j: (0, 0)),
                     wg_spec, wu_spec, pl.BlockSpec((CONV_W, bn), lambda i, j: (0, j)),
                     pl.BlockSpec((1, bn), lambda i, j: (0, j)), e_spec]),
        out_specs=[pl.BlockSpec((bm, bn), lambda i, j: (i, j)), tail_spec],
        out_shape=[jax.ShapeDtypeStruct((m, f), BF16), tail_shape],
        scratch_shapes=[pltpu.VMEM((bm, d), BF16)],
        compiler_params=_cparams(("parallel", "arbitrary")),
        name="ffn_in",
    )(x, *lead, gain.reshape(1, d), w_gate, w_up, conv_w, conv_b.reshape(1, f), e)
    if seq_in_block:
        state = tail.reshape(bsz, t_len, f)[:, t_len - 2:]
    else:
        state = tail.reshape(bsz, t_len // bm, 8, f)[:, -1, 6:]
    return hid, state


def _dot(a, b, dims=((1,), (0,))):
    return lax.dot_general(a.astype(BF16), b.astype(BF16), (dims, ((), ())), preferred_element_type=F32)


_NT = ((1,), (1,))
_TN = ((0,), (0,))


def _wkv_chunk_kernel(r_ref, ld_ref, k_ref, v_ref, kk_ref, as_ref, s0_ref, sall_ref, y_ref, sout_ref, s_sc, *,
                      c, hb, n, ng, nb):
    del sall_ref
    w = hb * n
    rb = hb * c
    rows = nb * rb
    first = pl.program_id(2) == 0
    last = pl.program_id(2) == pl.num_programs(2) - 1
    ri = lax.broadcasted_iota(jnp.int32, (rb, w), 0)
    ci = lax.broadcasted_iota(jnp.int32, (rb, w), 1)
    head_mask = (ri // c) == (ci // n)
    bd = lambda z: jnp.where(head_mask, jnp.concatenate([z] * hb, axis=0), 0.0)
    trow = lax.broadcasted_iota(jnp.int32, (c, w), 0)
    tr = lax.broadcasted_iota(jnp.int32, (rows, rows), 0)
    tc_ = lax.broadcasted_iota(jnp.int32, (rows, rows), 1)
    same = (tr // c) == (tc_ // c)
    strict = same & ((tr % c) > (tc_ % c))
    incl = same & ((tr % c) >= (tc_ % c))
    eye = jnp.where(tr == tc_, 1.0, 0.0)
    grp = range(ng)
    bat = range(nb)
    lanes = [slice(gi * w, (gi + 1) * w) for gi in grp]
    cat = lambda parts: parts[0] if len(parts) == 1 else jnp.concatenate(parts, axis=0)
    rsl = lambda z, bi: z[bi * rb:(bi + 1) * rb]

    @pl.when(first)
    def _():
        sr = lax.broadcasted_iota(jnp.int32, (w, w), 0)
        sc_ = lax.broadcasted_iota(jnp.int32, (w, w), 1)
        for gi in grp:
            for bi in bat:
                s0 = jnp.concatenate([s0_ref[bi, gi * hb + h] for h in range(hb)], axis=0)
                s_sc[gi * nb + bi] = jnp.where((sr // n) == (sc_ // n),
                                               jnp.concatenate([s0] * hb, axis=1), 0.0)

    def prep(gi):
        parts = []
        for bi in bat:
            r, ld, k, v, kk, asig = (ref[bi, :, lanes[gi]]
                                     for ref in (r_ref, ld_ref, k_ref, v_ref, kk_ref, as_ref))
            lam = ld
            shift = 1
            while shift < c:
                lam = lam + jnp.where(trow >= shift, pltpu.roll(lam, shift, axis=0), 0.0)
                shift *= 2
            g_inv = jnp.exp(-lam)
            parts.append((bd(-kk * jnp.exp(lam - ld)), bd(r * jnp.exp(lam)), bd(kk * asig * g_inv),
                          bd(k * g_inv), bd(v), jnp.exp(lam[c - 1:c])))
        return tuple(cat([pt[i] for pt in parts]) for i in range(5)) + ([pt[5] for pt in parts],)

    a_bd, r_bd, b_bd, k_bd, v_bd, g_end = zip(*[prep(gi) for gi in grp])
    p = [_dot(jnp.concatenate([a_bd[gi], r_bd[gi]], axis=0),
              jnp.concatenate([b_bd[gi], k_bd[gi]], axis=0), _NT) for gi in grp]
    l_pow = [jnp.where(strict, p[gi][:rows, :rows], 0.0) for gi in grp]
    t_inv = [eye + l_pow[gi] for gi in grp]
    for _ in range(c.bit_length() - 2):
        l_pow = [_dot(l_pow[gi], l_pow[gi]) for gi in grp]
        t_inv = [t_inv[gi] + _dot(t_inv[gi], l_pow[gi]) for gi in grp]
    s = [[s_sc[gi * nb + bi] for bi in bat] for gi in grp]
    xs = [[_dot(jnp.concatenate([rsl(a_bd[gi], bi), rsl(r_bd[gi], bi)], axis=0), s[gi][bi], _NT)
           for bi in bat] for gi in grp]
    xs_a = [cat([xs[gi][bi][:rb] for bi in bat]) for gi in grp]
    xs_r = [cat([xs[gi][bi][rb:] for bi in bat]) for gi in grp]
    lkv = [_dot(jnp.where(strict, p[gi][:rows, rows:], 0.0), v_bd[gi]) for gi in grp]
    pkv = [_dot(jnp.where(incl, p[gi][rows:, rows:], 0.0), v_bd[gi]) for gi in grp]
    u = [_dot(t_inv[gi], xs_a[gi] + lkv[gi]) for gi in grp]
    y_bd = [xs_r[gi] + pkv[gi] + _dot(jnp.where(incl, p[gi][rows:, :rows], 0.0), u[gi]) for gi in grp]
    s_new = [[s[gi][bi] * g_end[gi][bi]
              + _dot(jnp.concatenate([rsl(u[gi], bi), rsl(v_bd[gi], bi)], axis=0),
                     jnp.concatenate([rsl(b_bd[gi], bi), rsl(k_bd[gi], bi)], axis=0) * g_end[gi][bi], _TN)
              for bi in bat] for gi in grp]
    for gi in grp:
        for bi in bat:
            y = y_bd[gi][bi * rb:bi * rb + c]
            for h in range(1, hb):
                y = y + y_bd[gi][bi * rb + h * c:bi * rb + (h + 1) * c]
            y_ref[bi, :, lanes[gi]] = y
            s_sc[gi * nb + bi] = s_new[gi][bi]

    @pl.when(last)
    def _():
        for gi in grp:
            for bi in bat:
                for h in range(hb):
                    sout_ref[bi, gi * hb + h] = s_new[gi][bi][h * n:(h + 1) * n, h * n:(h + 1) * n]


def wkv_chunked(r, ld, k, v, kk, asig, s0, n_heads, s_all, layer):
    bsz, t, d = r.shape
    n = d // n_heads
    hb = _pick(n_heads, (4, 2, 1))
    ng = _pick(n_heads // hb, (4, 2, 1))
    c = _pick(t, (WKV_CHUNK, 32, 16, 8))
    nb = _pick(bsz, tuple(x for x in (8, 4, 2, 1) if x * hb * c <= MXU_WIDTH))
    seq_spec = pl.BlockSpec((nb, c, ng * hb * n), lambda b, q, ch: (b, ch, q))
    st_spec = pl.BlockSpec((nb, ng * hb, n, n), lambda b, q, ch: (b, q, 0, 0))
    st_out_spec = pl.BlockSpec((None, nb, ng * hb, n, n), lambda b, q, ch: (layer, b, q, 0, 0))
    return pl.pallas_call(
        functools.partial(_wkv_chunk_kernel, c=c, hb=hb, n=n, ng=ng, nb=nb),
        grid=(bsz // nb, n_heads // (hb * ng), t // c),
        in_specs=[seq_spec] * 6 + [st_spec, pl.BlockSpec(memory_space=pl.ANY)],
        out_specs=[seq_spec, st_out_spec],
        out_shape=[jax.ShapeDtypeStruct((bsz, t, d), F32),
                   jax.ShapeDtypeStruct(s_all.shape, F32)],
        scratch_shapes=[pltpu.VMEM((ng * nb, hb * n, hb * n), F32)],
        input_output_aliases={7: 1},
        compiler_params=_cparams(("parallel", "parallel", "arbitrary")),
        name="wkv_chunked",
    )(r, ld, k, v, kk, asig, s0, s_all)


def _head_sum(x, n):
    wd = min(MXU_WIDTH, x.shape[1])
    gi = lax.broadcasted_iota(jnp.int32, (wd, wd), 0) // n
    gj = lax.broadcasted_iota(jnp.int32, (wd, wd), 1) // n
    g = jnp.where(gi == gj, 1.0, 0.0).astype(BF16)
    hi = x.astype(BF16)
    lo = (x - hi.astype(F32)).astype(BF16)
    parts = []
    for s in range(0, x.shape[1], wd):
        parts.append(jnp.dot(hi[:, s:s + wd], g, preferred_element_type=F32)
                     + jnp.dot(lo[:, s:s + wd], g, preferred_element_type=F32))
    return parts[0] if len(parts) == 1 else jnp.concatenate(parts, axis=1)


def _softplus(z):
    return jnp.maximum(z, 0.0) + jnp.log(1.0 + jnp.exp(-jnp.abs(z)))


def _rwkv_pre_kernel(*refs, n, has_v_lora):
    if has_v_lora:
        (k_ref, v_ref, hw_ref, ha_ref, w2_ref, a2_ref, w0_ref, a0_ref, kk_ref, ka_ref,
         vf_ref, hv_ref, v2_ref, v0_ref, ld_out, a_out, kk_out, k_out, v_out) = refs
    else:
        (k_ref, v_ref, hw_ref, ha_ref, w2_ref, a2_ref, w0_ref, a0_ref, kk_ref, ka_ref,
         ld_out, a_out, kk_out, k_out) = refs
    k = k_ref[...]
    w_pre = w0_ref[...] + jnp.dot(jnp.tanh(hw_ref[...]).astype(BF16), w2_ref[...], preferred_element_type=F32)
    ld_out[...] = -jnp.exp(-_softplus(-w_pre) - 0.5)
    a = jax.nn.sigmoid(a0_ref[...] + jnp.dot(ha_ref[...].astype(BF16), a2_ref[...], preferred_element_type=F32))
    a_out[...] = a
    kk = k * kk_ref[...]
    kk_out[...] = kk / jnp.maximum(jnp.sqrt(_head_sum(kk * kk, n)), 1e-12)
    k_out[...] = k * (1.0 + (a - 1.0) * ka_ref[...])
    if has_v_lora:
        v = v_ref[...]
        gate = jax.nn.sigmoid(v0_ref[...] + jnp.dot(hv_ref[...].astype(BF16), v2_ref[...],
                                                    preferred_element_type=F32))
        v_out[...] = v + (vf_ref[...] - v) * gate


def rwkv_pre(k, v, hw, ha, p, v_first, hv, n):
    m, d = k.shape
    bm = _pick(m, (256, 128, 64, 32, 16, 8))
    has_v_lora = p["v_lora"] is not None
    row = lambda width: pl.BlockSpec((bm, width), lambda i: (i, 0))
    full = lambda a: pl.BlockSpec(a.shape, lambda i: (0, 0))
    vec = lambda z: z.reshape(1, d)
    args = [k, v, hw, ha, p["w2"], p["a2"], vec(p["w0"]), vec(p["a0"]), vec(p["k_k"]), vec(p["k_a"])]
    specs = [row(d), row(d), row(hw.shape[1]), row(ha.shape[1])] + [full(a) for a in args[4:]]
    n_out = 4
    if has_v_lora:
        v0, _, v2 = p["v_lora"]
        extra = [v_first, hv, v2, vec(v0)]
        args += extra
        specs += [row(d), row(hv.shape[1]), full(v2), full(extra[3])]
        n_out = 5
    outs = pl.pallas_call(
        functools.partial(_rwkv_pre_kernel, n=n, has_v_lora=has_v_lora),
        grid=(m // bm,),
        in_specs=specs,
        out_specs=[row(d)] * n_out,
        out_shape=[jax.ShapeDtypeStruct((m, d), F32)] * n_out,
        compiler_params=_cparams(("parallel",)),
        name="rwkv_pre",
    )(*args)
    return tuple(outs) + ((v,) if not has_v_lora else ())


def _rwkv_post_kernel(y_ref, r_ref, k_ref, v_ref, hg_ref, g2_ref, lnw_ref, lnb_ref, rk_ref, o_ref, *, n):
    y = y_ref[...]
    mu = _head_sum(y, n) / n
    dlt = y - mu
    var = _head_sum(dlt * dlt, n) / n
    yn = dlt * lax.rsqrt(var + n * GN_EPS_PER_CHANNEL) * lnw_ref[...] + lnb_ref[...]
    bonus = _head_sum(r_ref[...] * k_ref[...] * rk_ref[...], n) * v_ref[...]
    g = jnp.dot(jax.nn.sigmoid(hg_ref[...]).astype(BF16), g2_ref[...], preferred_element_type=F32)
    o_ref[...] = ((yn + bonus) * g).astype(o_ref.dtype)


def rwkv_post(y, r, k, v, hg, p, n):
    m, d = y.shape
    bm = _pick(m, (256, 128, 64, 32, 16, 8))
    row = lambda width: pl.BlockSpec((bm, width), lambda i: (i, 0))
    full = lambda a: pl.BlockSpec(a.shape, lambda i: (0, 0))
    vec = lambda z: z.reshape(1, d)
    consts = [p["g2"], vec(p["ln_w"]), vec(p["ln_b"]), vec(p["r_k"])]
    return pl.pallas_call(
        functools.partial(_rwkv_post_kernel, n=n),
        grid=(m // bm,),
        in_specs=[row(d)] * 4 + [row(hg.shape[1])] + [full(a) for a in consts],
        out_specs=row(d),
        out_shape=jax.ShapeDtypeStruct((m, d), BF16),
        compiler_params=_cparams(("parallel",)),
        name="rwkv_post",
    )(y, r, k, v, hg, *consts)


def _flash_kernel(qn_ref, qr_ref, kn_ref, kr_ref, v_ref, o_ref, *, bq, n_q, hp, dh, dv, scale):
    qi = pl.program_id(2)
    dn = (((1,), (1,)), ((), ()))
    heads = range(hp)
    hs = [slice(h * dh, (h + 1) * dh) for h in heads]
    vs = [slice(h * dv, (h + 1) * dv) for h in heads]

    for i in range(n_q):
        @pl.when(qi == i)
        def _(i=i):
            kv_len = (i + 1) * bq
            past = kv_len - bq
            tri = (lax.broadcasted_iota(jnp.int32, (bq, bq), 1) <= lax.broadcasted_iota(jnp.int32, (bq, bq), 0))
            s = [(lax.dot_general(qn_ref[0, :, hs[h]], kn_ref[0, :kv_len, hs[h]], dn, preferred_element_type=F32)
                  + lax.dot_general(qr_ref[0, :, hs[h]], kr_ref[0, :kv_len, hs[h]], dn,
                                    preferred_element_type=F32)) * scale for h in heads]
            s_diag = [jnp.where(tri, s[h][:, past:], -jnp.inf) for h in heads]
            m = [jnp.max(s_diag[h], axis=-1, keepdims=True) for h in heads]
            if past:
                m = [jnp.maximum(m[h], jnp.max(s[h][:, :past], axis=-1, keepdims=True)) for h in heads]
            p_diag = [jnp.exp(s_diag[h] - m[h]) for h in heads]
            l = [jnp.sum(p_diag[h], axis=-1, keepdims=True) for h in heads]
            o = [jnp.dot(p_diag[h].astype(BF16), v_ref[0, past:kv_len, vs[h]], preferred_element_type=F32)
                 for h in heads]
            if past:
                p_past = [jnp.exp(s[h][:, :past] - m[h]) for h in heads]
                l = [l[h] + jnp.sum(p_past[h], axis=-1, keepdims=True) for h in heads]
                o = [o[h] + jnp.dot(p_past[h].astype(BF16), v_ref[0, :past, vs[h]], preferred_element_type=F32)
                     for h in heads]
            for h in heads:
                o_ref[0, :, vs[h]] = (o[h] / l[h]).astype(o_ref.dtype)


def flash_prompt(qn, qr, kn, kr, v, n_heads, scale):
    bsz, s_len, _ = qn.shape
    dh = qn.shape[-1] // n_heads
    dv = v.shape[-1] // n_heads
    bq = _pick(s_len, (512, 256, 128, 64, 32, 16, 8))
    n_q = s_len // bq
    hp = _pick(n_heads, (4, 2, 1))
    q_spec = pl.BlockSpec((1, bq, hp * dh), lambda b, h, i: (b, i, h))
    k_spec = pl.BlockSpec((1, s_len, hp * dh), lambda b, h, i: (b, 0, h))
    return pl.pallas_call(
        functools.partial(_flash_kernel, bq=bq, n_q=n_q, hp=hp, dh=dh, dv=dv, scale=scale),
        grid=(bsz, n_heads // hp, n_q),
        in_specs=[q_spec, q_spec, k_spec, k_spec,
                  pl.BlockSpec((1, s_len, hp * dv), lambda b, h, i: (b, 0, h))],
        out_specs=pl.BlockSpec((1, bq, hp * dv), lambda b, h, i: (b, i, h)),
        out_shape=jax.ShapeDtypeStruct((bsz, s_len, n_heads * dv), BF16),
        compiler_params=_cparams(("parallel", "parallel", "arbitrary")),
        name="flash_prompt",
    )(qn, qr, kn, kr, v)


def _mla_head_kernel(*refs, n_heads, d_qk, half, is_query):
    tile = lambda z: jnp.concatenate([z] * n_heads, axis=1)
    if is_query:
        xn_ref, xr_ref, cos_ref, sin_a_ref, sin_b_ref, gn_ref, gr_ref, on_ref, or_ref = refs
        xr = xr_ref[...]
        width = xr.shape[1]
        xr = (xr * tile(cos_ref[...]) + pltpu.roll(xr, width - half, axis=1) * tile(sin_a_ref[...])
              + pltpu.roll(xr, half, axis=1) * tile(sin_b_ref[...]))
    else:
        xn_ref, kpe_ref, gn_ref, gr_ref, on_ref, or_ref = refs
        kpe = kpe_ref[...]
        pad = jnp.zeros((kpe.shape[0], xn_ref.shape[1] // n_heads - kpe.shape[1]), F32)
        xr = tile(jnp.concatenate([kpe, pad], axis=1))
    xn = xn_ref[...]
    inv = lax.rsqrt(_head_sum(xn * xn + xr * xr, xn.shape[1] // n_heads) / d_qk + RMS_EPS)
    on_ref[...] = (xn * inv * gn_ref[...]).astype(on_ref.dtype)
    or_ref[...] = (xr * inv * gr_ref[...]).astype(or_ref.dtype)


def mla_head_norm(x, gain_n, gain_r, n_heads, d_qk, d_rope, rope_tabs=None, k_pe=None):
    m = x.shape[0]
    is_query = rope_tabs is not None
    wd = gain_n.shape[1]
    dh = wd // n_heads
    bm = _pick(m, (256, 128, 64, 32, 16, 8))
    row = lambda width, blk=0: pl.BlockSpec((bm, width), lambda i, blk=blk: (i, blk))
    full = lambda a: pl.BlockSpec(a.shape, lambda i: (0, 0))
    if is_query:
        args = [x, x, *rope_tabs, gain_n, gain_r]
        specs = [row(wd, 0), row(wd, 1), row(dh), row(dh), row(dh), full(gain_n), full(gain_r)]
    else:
        args = [x, k_pe, gain_n, gain_r]
        specs = [row(wd), row(k_pe.shape[1]), full(gain_n), full(gain_r)]
    return pl.pallas_call(
        functools.partial(_mla_head_kernel, n_heads=n_heads, d_qk=d_qk, half=d_rope // 2, is_query=is_query),
        grid=(m // bm,),
        in_specs=specs,
        out_specs=[row(wd), row(wd)],
        out_shape=[jax.ShapeDtypeStruct((m, wd), BF16)] * 2,
        compiler_params=_cparams(("parallel",)),
        name="mla_head_norm",
    )(*args)


def _kscale_kernel(pt_ref, *refs, npg, n_heads, d_qk):
    del pt_ref
    c_refs = refs[:npg]
    kpe_refs = refs[npg:2 * npg]
    wt_ref = refs[2 * npg]
    s_ref, c_out, kpe_out = refs[2 * npg + 1:]
    page = c_refs[0].shape[1]
    pair = 2 if npg % 2 == 0 else 1
    for p in range(0, npg, pair):
        keys = slice(p * page, (p + pair) * page)
        c = jnp.concatenate([c_refs[p + q][0] for q in range(pair)], axis=0).astype(BF16)
        kpe_t = jnp.concatenate([kpe_refs[p + q][0] for q in range(pair)], axis=1)
        k_t = lax.dot_general(wt_ref[...], c, (((1,), (1,)), ((), ())), preferred_element_type=F32)
        sq = k_t * k_t
        ss = jnp.sum(sq.reshape(n_heads, sq.shape[0] // n_heads, sq.shape[1]), axis=1)
        pe = jnp.sum(kpe_t * kpe_t, axis=0, keepdims=True)
        s_ref[0, :, keys] = lax.rsqrt((ss + pe) / d_qk + RMS_EPS)
        c_out[0, keys, :] = c
        kpe_out[0, :, keys] = kpe_t.astype(BF16)


def paged_key_scale(cache_c, cache_kpe_t, page_table_flat, w_uk_t, n_heads, d_qk, npg):
    n_used = page_table_flat.shape[0]
    page, d_c = cache_c.shape[1:]
    d_r = cache_kpe_t.shape[1]
    n_grp = n_used // npg
    nk = npg * page

    def page_spec(shape, p):
        return pl.BlockSpec((1,) + shape, lambda g, pt, p=p: (pt[g * npg + p], 0, 0))

    in_specs = ([page_spec((page, d_c), p) for p in range(npg)] + [page_spec((d_r, page), p) for p in range(npg)]
                + [pl.BlockSpec(w_uk_t.shape, lambda g, pt: (0, 0))])
    out = lambda shape: pl.BlockSpec((1,) + shape, lambda g, pt: (g, 0, 0))
    grid_spec = pltpu.PrefetchScalarGridSpec(
        num_scalar_prefetch=1, grid=(n_grp,), in_specs=in_specs,
        out_specs=[out((n_heads, nk)), out((nk, d_c)), out((d_r, nk))])
    return pl.pallas_call(
        functools.partial(_kscale_kernel, npg=npg, n_heads=n_heads, d_qk=d_qk),
        grid_spec=grid_spec,
        out_shape=[jax.ShapeDtypeStruct((n_grp, n_heads, nk), F32),
                   jax.ShapeDtypeStruct((n_grp, nk, d_c), BF16),
                   jax.ShapeDtypeStruct((n_grp, d_r, nk), BF16)],
        compiler_params=_cparams(("arbitrary",)),
        name="paged_key_scale",
    )(page_table_flat, *([cache_c] * npg), *([cache_kpe_t] * npg), w_uk_t)


def _paged_attn_kernel(ql_ref, qr_ref, cnew_ref, pnew_ref, snew_ref, c_ref, p_ref, s_ref, o_ref,
                       m_sc, l_sc, acc_sc, *, bb, n_heads, t_len, scale):
    g = pl.program_id(1)
    rows = n_heads * t_len
    dn = (((1,), (1,)), ((), ()))

    def process(c, kpe_t, key_scale, mask):
        bs = range(bb)
        sc = [lax.dot_general(ql_ref[b], c[b], dn, preferred_element_type=F32)
              + jnp.dot(qr_ref[b], kpe_t[b], preferred_element_type=F32) for b in bs]
        sc = [sc[b] * (jnp.concatenate([key_scale[b]] * t_len, axis=0) * scale) for b in bs]
        if mask is not None:
            sc = [jnp.where(mask, sc[b], -jnp.inf) for b in bs]
        m_new = [jnp.maximum(m_sc[b], jnp.max(sc[b], axis=-1, keepdims=True)) for b in bs]
        alpha = [jnp.exp(m_sc[b] - m_new[b]) for b in bs]
        p = [jnp.exp(sc[b] - m_new[b]) for b in bs]
        pv = [jnp.dot(p[b].astype(BF16), c[b], preferred_element_type=F32) for b in bs]
        for b in bs:
            l_sc[b] = l_sc[b] * alpha[b] + jnp.sum(p[b], axis=-1, keepdims=True)
            acc_sc[b] = acc_sc[b] * alpha[b] + pv[b]
            m_sc[b] = m_new[b]

    @pl.when(g == 0)
    def _():
        m_sc[...] = jnp.full(m_sc.shape, -jnp.inf, F32)
        l_sc[...] = jnp.zeros(l_sc.shape, F32)
        acc_sc[...] = jnp.zeros(acc_sc.shape, F32)
        nk = cnew_ref.shape[1]
        t_row = lax.broadcasted_iota(jnp.int32, (rows, nk), 0) // n_heads
        key = lax.broadcasted_iota(jnp.int32, (rows, nk), 1)
        process([cnew_ref[b] for b in range(bb)], [pnew_ref[b] for b in range(bb)],
                [snew_ref[b] for b in range(bb)], key <= t_row)

    process([c_ref[b, 0] for b in range(bb)], [p_ref[b, 0] for b in range(bb)],
            [s_ref[b, 0] for b in range(bb)], None)

    @pl.when(g == pl.num_programs(1) - 1)
    def _():
        for b in range(bb):
            o_ref[b] = acc_sc[b] / l_sc[b]


def paged_attention(q_lat, q_r, c_new, kpe_new_t, s_new, c_past, kpe_past_t, s_past, n_heads, t_len, scale):
    bsz, rows, d_c = q_lat.shape
    d_r = q_r.shape[2]
    _, n_grp, nk, _ = c_past.shape
    kn = c_new.shape[1]
    bb = _pick(bsz, (2, 1))

    def per_b(shape):
        return pl.BlockSpec((bb,) + shape, lambda b, g: (b, 0, 0))

    def per_g(shape):
        return pl.BlockSpec((bb, 1) + shape, lambda b, g: (b, g, 0, 0))

    return pl.pallas_call(
        functools.partial(_paged_attn_kernel, bb=bb, n_heads=n_heads, t_len=t_len, scale=scale),
        grid=(bsz // bb, n_grp),
        in_specs=[per_b((rows, d_c)), per_b((rows, d_r)), per_b((kn, d_c)), per_b((d_r, kn)),
                  per_b((n_heads, kn)), per_g((nk, d_c)), per_g((d_r, nk)), per_g((n_heads, nk))],
        out_specs=per_b((rows, d_c)),
        out_shape=jax.ShapeDtypeStruct((bsz, rows, d_c), F32),
        scratch_shapes=[pltpu.VMEM((bb, rows, 1), F32), pltpu.VMEM((bb, rows, 1), F32),
                        pltpu.VMEM((bb, rows, d_c), F32)],
        compiler_params=_cparams(("parallel", "arbitrary")),
        name="paged_attention",
    )(q_lat, q_r, c_new, kpe_new_t, s_new, c_past, kpe_past_t, s_past)


def _rms_norm(x, g):
    return x * lax.rsqrt(jnp.mean(x * x, axis=-1, keepdims=True) + RMS_EPS) * g


def _rope(x, pos):
    half = x.shape[-1] // 2
    inv = ROPE_BASE ** (-jnp.arange(half, dtype=F32) / half)
    ang = pos.astype(F32)[:, None] * inv[None, :]
    cos = jnp.cos(ang)[None, :, None, :]
    sin = jnp.sin(ang)[None, :, None, :]
    x1, x2 = x[..., :half], x[..., half:]
    return jnp.concatenate([x1 * cos - x2 * sin, x1 * sin + x2 * cos], axis=-1)


def _conv_ffn(x, prev, p):
    hid, state = ffn_in(x, p["norm"], prev, p["w_gate"], p["w_up"], p["conv_w"], p["conv_b"])
    return mm(hid, p["w_down"], res=x.reshape(hid.shape[0], -1)).reshape(x.shape), state


def _rwkv7_time_mix(x, shift_prev, wkv_prev, v_first, p, wkv_all, layer):
    bsz, t, d = x.shape
    n_heads, n = p["r_k"].shape
    h = _rms_norm(x, p["norm"])
    prev = jnp.concatenate([shift_prev[:, None, :], h[:, :-1]], axis=1)
    xx = prev - h
    xr, xw, xk, xv, xa, xg = ((h + xx * p["mix"][i]).astype(BF16) for i in range(6))
    flat = lambda z: z.reshape(bsz * t, -1)
    r = mm(flat(xr), p["wr"])
    k = mm(flat(xk), p["wk"])
    v = mm(flat(xv), p["wv"])
    hv = None if p["v_lora"] is None else mm(flat(xv), p["v_lora"][1])
    log_decay, a, kk, k, v = rwkv_pre(k, v, mm(flat(xw), p["w1"]), mm(flat(xa), p["a1"]), p, v_first, hv, n)
    if p["v_lora"] is None:
        v_first = v
    seq = lambda z: z.reshape(bsz, t, d)
    y, s = wkv_chunked(seq(r), seq(log_decay), seq(k), seq(v), seq(kk), seq(a), wkv_prev, n_heads,
                       wkv_all, layer)
    out = rwkv_post(flat(y), r, k, v, mm(flat(xg), p["g1"]), p, n)
    return mm(out, p["wo"], res=flat(x)).reshape(x.shape), h[:, -1], s, v_first


def _mla_kv_side(x, pos, kv):
    d_c = kv["norm_c"].shape[0]
    ckr = mm(x, kv["w_dkv_kr"], norm_gain=kv["norm"])
    c = _rms_norm(ckr[..., :d_c], kv["norm_c"])
    k_pe = _rope(ckr[..., d_c:][:, :, None, :], pos)[:, :, 0, :]
    return c, k_pe


def _head_gains(g, n_heads, d_nope, other=None):
    if other is not None:
        g = g * other
    g_r = jnp.pad(g[d_nope:], (0, 2 * d_nope - g.shape[0]))
    return jnp.tile(g[:d_nope], n_heads)[None], jnp.tile(g_r, n_heads)[None]


def _mla_query(x, pos, p, kv, n_heads, d_nope, d_rope, absorb_key_gain):
    bsz, t, _ = x.shape
    cq = mm(x.reshape(bsz * t, -1), p["w_dq"], norm_gain=p["norm"])
    q = mm(cq, p["w_uq"], norm_gain=p["norm_cq"])
    half = d_rope // 2
    inv = ROPE_BASE ** (-jnp.arange(half, dtype=F32) / half)
    ang = pos.astype(F32)[:, None] * inv[None, :]
    cos, sin, zero = jnp.cos(ang), jnp.sin(ang), jnp.zeros((t, half), F32)
    lane_pad = jnp.zeros((t, d_nope - d_rope), F32)
    tabs = tuple(jnp.tile(jnp.concatenate(parts + [lane_pad], axis=1), (bsz, 1))
                 for parts in ([cos, cos], [-sin, zero], [zero, sin]))
    g_n, g_r = _head_gains(p["norm_q"], n_heads, d_nope, kv["norm_k"] if absorb_key_gain else None)
    return mla_head_norm(q, g_n, g_r, n_heads, d_nope + d_rope, d_rope, rope_tabs=tabs)


def _prompt_attend_fn(c, k_pe, kv, n_heads, d_nope):
    bsz, s_len, _ = c.shape
    d_rope = k_pe.shape[-1]
    d_qk = d_nope + d_rope
    g_n, g_r = _head_gains(kv["norm_k"], n_heads, d_nope)
    k_n, k_r = mla_head_norm(mm(c.reshape(bsz * s_len, -1), kv["w_uk_2d"]), g_n, g_r, n_heads, d_qk, d_rope,
                             k_pe=k_pe.reshape(bsz * s_len, d_rope))
    v = mm(c, kv["w_uv_2d"], BF16)
    seq = lambda z: z.reshape(bsz, s_len, -1)

    def attend(q):
        return flash_prompt(seq(q[0]), seq(q[1]), seq(k_n), seq(k_r), v, n_heads, d_qk ** -0.5)

    return attend


def _sample_attend_fn(c_new, kpe_new, cache_c, cache_kpe, page_table, kv, n_heads, d_nope):
    bsz, t, d_c = c_new.shape
    d_r = kpe_new.shape[-1]
    d_qk = d_nope + d_r
    page = cache_c.shape[1]
    pt_flat = page_table.reshape(-1)
    npg = _pick(page_table.shape[1], (PAGES_PER_STEP, 8, 4, 2, 1))

    k_nope_new = mm(c_new, kv["w_uk_2d"]).reshape(bsz, t, n_heads, d_nope)
    ss = jnp.sum(k_nope_new * k_nope_new, axis=-1) + jnp.sum(kpe_new * kpe_new, axis=-1)[..., None]
    s_new = lax.rsqrt(ss / d_qk + RMS_EPS)
    kn = page
    c_new_p = jnp.pad(c_new.astype(BF16), ((0, 0), (0, kn - t), (0, 0)))
    kpe_new_t = jnp.pad(kpe_new.astype(BF16).transpose(0, 2, 1), ((0, 0), (0, 0), (0, kn - t)))
    s_new_p = jnp.pad(s_new.transpose(0, 2, 1), ((0, 0), (0, 0), (0, kn - t)), constant_values=1.0)

    s_past, c_past, kpe_past_t = paged_key_scale(cache_c, cache_kpe.transpose(0, 2, 1), pt_flat, kv["w_uk_t"],
                                                 n_heads, d_qk, npg)
    n_grp = page_table.shape[1] // npg
    nk = npg * page
    s_past = s_past.reshape(bsz, n_grp, n_heads, nk)
    c_past = c_past.reshape(bsz, n_grp, nk, d_c)
    kpe_past_t = kpe_past_t.reshape(bsz, n_grp, d_r, nk)
    scale = d_qk ** -0.5

    def attend(q):
        q_lat = hmm(q[0], kv["w_uk_hdc"], BF16).reshape(bsz, t * n_heads, d_c)
        q_r = q[1].reshape(bsz, t * n_heads, d_nope)[..., :d_r]
        o_lat = paged_attention(q_lat, q_r, c_new_p, kpe_new_t, s_new_p, c_past, kpe_past_t, s_past,
                                n_heads, t, scale)
        o = hmm(o_lat.reshape(bsz * t, n_heads * d_c), kv["w_uv_hcd"])
        return o.reshape(bsz, t, -1)

    return attend


def kernel(x_prompt, x_sample, cache_kv_latent, cache_k_pe, state_wkv, state_shift, state_conv, page_table,
           ffn_norm, ffn_w_gate, ffn_w_up, ffn_conv_w, ffn_conv_b, ffn_w_down,
           rw_norm, rw_mix, rw_w0, rw_w1, rw_w2, rw_a0, rw_a1, rw_a2, rw_v0, rw_v1, rw_v2,
           rw_g1, rw_g2, rw_k_k, rw_k_a, rw_r_k, rw_wr, rw_wk, rw_wv, rw_ln_w, rw_ln_b, rw_wo,
           kv_norm, kv_w_dkv, kv_norm_c, kv_w_kr, kv_w_uk, kv_w_uv, kv_norm_k,
           mla_norm, mla_w_dq, mla_norm_cq, mla_w_uq, mla_norm_q, mla_wo):
    depth = ffn_norm.shape[0]
    n_a = rw_norm.shape[0]
    d_c, n_heads, d_nope = kv_w_uk.shape
    d_v = kv_w_uv.shape[2]
    bf = lambda z: z.astype(BF16)

    w_gate_b, w_up_b, w_down_b = bf(ffn_w_gate), bf(ffn_w_up), bf(ffn_w_down)
    wr_b, wk_b, wv_b, wo_b = bf(rw_wr), bf(rw_wk), bf(rw_wv), bf(rw_wo)
    ffn = [dict(norm=ffn_norm[l], w_gate=(w_gate_b, l), w_up=(w_up_b, l), conv_w=ffn_conv_w[l],
                conv_b=ffn_conv_b[l], w_down=(w_down_b, l)) for l in range(depth)]
    rw = [dict(norm=rw_norm[l], mix=rw_mix[l], w0=rw_w0[l], w1=bf(rw_w1[l]), w2=bf(rw_w2[l]),
               a0=rw_a0[l], a1=bf(rw_a1[l]), a2=bf(rw_a2[l]), g1=bf(rw_g1[l]), g2=bf(rw_g2[l]),
               k_k=rw_k_k[l], k_a=rw_k_a[l], r_k=rw_r_k[l], wr=(wr_b, l), wk=(wk_b, l),
               wv=(wv_b, l), ln_w=rw_ln_w[l], ln_b=rw_ln_b[l], wo=(wo_b, l),
               v_lora=None if l == 0 else (rw_v0[l - 1], bf(rw_v1[l - 1]), bf(rw_v2[l - 1])))
          for l in range(n_a)]
    kv = dict(norm=kv_norm, norm_c=kv_norm_c, norm_k=kv_norm_k,
              w_dkv_kr=bf(jnp.concatenate([kv_w_dkv, kv_w_kr], axis=1)),
              w_uk_2d=bf(kv_w_uk.reshape(d_c, n_heads * d_nope)),
              w_uv_2d=bf(kv_w_uv.reshape(d_c, n_heads * d_v)),
              w_uk_t=bf(kv_w_uk.reshape(d_c, n_heads * d_nope).T),
              w_uk_hdc=bf(kv_w_uk.transpose(1, 2, 0)),
              w_uv_hcd=bf(kv_w_uv.transpose(1, 0, 2)))
    d_qc = mla_w_uq.shape[1]
    d_rope = kv_w_kr.shape[1]

    def split_heads(w_uq):
        w_r = jnp.pad(w_uq[:, :, d_nope:], ((0, 0), (0, 0), (0, d_nope - d_rope)))
        return jnp.concatenate([w_uq[:, :, :d_nope].reshape(d_qc, -1), w_r.reshape(d_qc, -1)], axis=1)

    w_dq_b, mla_wo_b = bf(mla_w_dq), bf(mla_wo)
    mla = [dict(norm=mla_norm[b], w_dq=(w_dq_b, b), norm_cq=mla_norm_cq[b],
                w_uq=bf(split_heads(mla_w_uq[b])), norm_q=mla_norm_q[b],
                wo=(mla_wo_b, b)) for b in range(depth - n_a)]

    def trunk(x, pos, shift_in, wkv_in, conv_in, make_attend, absorb_key_gain):
        v_first = attend = c = k_pe = None
        shifts, convs = [], []
        wkvs = jnp.zeros(wkv_in.shape, F32)
        for layer in range(depth):
            if layer < n_a:
                x, sh, wkvs, v_first = _rwkv7_time_mix(x, shift_in[layer], wkv_in[layer], v_first, rw[layer],
                                                       wkvs, layer)
                shifts.append(sh)
            else:
                if layer == n_a:
                    c, k_pe = _mla_kv_side(x, pos, kv)
                    attend = make_attend(c, k_pe)
                p = mla[layer - n_a]
                q = _mla_query(x, pos, p, kv, n_heads, d_nope, d_rope, absorb_key_gain)
                x = mm(attend(q), p["wo"], res=x)
            x, cv = _conv_ffn(x, conv_in[layer], ffn[layer])
            convs.append(cv)
        return x, c, k_pe, wkvs, jnp.stack(shifts), jnp.stack(convs)

    bp, sp, d = x_prompt.shape
    n_rw_heads, n = rw_r_k.shape[1:]
    d_ff = ffn_w_gate.shape[2]
    out_p = trunk(
        x_prompt, jnp.arange(sp),
        jnp.zeros((n_a, bp, d), F32), jnp.zeros((n_a, bp, n_rw_heads, n, n), F32),
        jnp.zeros((depth, bp, CONV_W - 1, d_ff), F32),
        lambda c, kp: _prompt_attend_fn(c, kp, kv, n_heads, d_nope), False)

    past_len = page_table.shape[1] * cache_kv_latent.shape[1]
    out_s = trunk(
        x_sample, past_len + jnp.arange(x_sample.shape[1]), state_shift, state_wkv, state_conv,
        lambda c, kp: _sample_attend_fn(c, kp, cache_kv_latent, cache_k_pe, page_table, kv, n_heads, d_nope),
        True)

    return (out_p[0], out_s[0]) + tuple(out_p[1:]) + tuple(out_s[1:])
```
